```python
import math
import jax
import jax.numpy as jnp
from jax import lax
import numpy as np

D_MODEL = 1024
BATCH = 16
SEQ = 4096
DEPTH = 2
DEC_BATCH = 16
DEC_SEQ = 2048
PAST_LEN = 128

PLE_DIM = 256
HG_HEADS = 4
HG_DK = 64
HG_DV = 64
HG_WIDTH = HG_HEADS * HG_DV
HG_CHUNK = 64
LRU_WIDTH = D_MODEL // 2
LRU_BLOCKS = 8
LRU_BLOCK = LRU_WIDTH // LRU_BLOCKS
LRU_C = 8.0
CONV_W = 4
DA_HEADS = 4
DA_HEAD_QK = 32
DA_HEAD_V = 2 * DA_HEAD_QK
DA_WIDTH = DA_HEADS * DA_HEAD_V
ROPE_THETA = 500000.0
ROPE_DIM = DA_HEAD_QK // 4
Q_BLOCK = 128
D_MIX = HG_WIDTH + LRU_WIDTH + DA_WIDTH
RMS_EPS = 1e-6
SPLIT_SIZES = (HG_HEADS * HG_DK, HG_HEADS * HG_DK, HG_HEADS * HG_DK, HG_WIDTH, HG_WIDTH,
               LRU_WIDTH, LRU_WIDTH,
               2 * DA_HEADS * DA_HEAD_QK, 2 * DA_HEADS * DA_HEAD_QK, DA_WIDTH, DA_WIDTH)
D_IN = sum(SPLIT_SIZES)

kernel_name = 'hybrid_bidir_hgrn2_rglru_diffattn'


def rmsnorm(x, g):
    x32 = x.astype(jnp.float32)
    y = x32 * lax.rsqrt(jnp.mean(x32 * x32, axis=-1, keepdims=True) + RMS_EPS)
    return (y * g.astype(jnp.float32)).astype(x.dtype)


def hgrn2_scan(q, logf, k, v):
    B, S, H, K = q.shape
    V = v.shape[-1]
    C = HG_CHUNK
    N = S // C

    def chunks(t):
        return t.reshape(B, N, C, H, t.shape[-1]).transpose(1, 0, 3, 2, 4)

    qc, kc, vc = chunks(q), chunks(k), chunks(v)
    cc = jnp.cumsum(chunks(logf), axis=3)
    causal = jnp.tril(jnp.ones((C, C), dtype=bool))[:, :, None]

    def step(state, inp):
        qb, cb, kb, vb = inp
        diff = cb[:, :, :, None, :] - cb[:, :, None, :, :]
        decay = jnp.where(causal, jnp.exp(jnp.where(causal, diff, 0.0)), 0.0)
        scores = jnp.einsum('bhik,bhjk,bhijk->bhij', qb, kb, decay)
        o = (jnp.einsum('bhij,bhjv->bhiv', scores, vb)
             + jnp.einsum('bhik,bhkv->bhiv', qb * jnp.exp(cb), state))
        c_last = cb[:, :, -1:, :]
        new_state = (jnp.exp(c_last)[:, :, 0, :, None] * state
                     + jnp.einsum('bhjk,bhjv->bhkv', kb * jnp.exp(c_last - cb), vb))
        return new_state, o

    init = jnp.zeros((B, H, K, V), dtype=jnp.float32)
    _, o = lax.scan(step, init, (qc, cc, kc, vc))
    return o.transpose(1, 0, 3, 2, 4).reshape(B, S, H, V)


def hgrn2_branch(zq, zf_fwd, zf_bwd, zi, zg, lb, norm_g):
    B, S, _ = zq.shape

    def heads(t):
        return t.astype(jnp.float32).reshape(B, S, HG_HEADS, -1)

    q = jax.nn.silu(heads(zq))
    v = heads(zi)

    def gates(zf, lb_d):
        zf = heads(zf)
        lb_d = lb_d.reshape(HG_HEADS, HG_DK)
        f = lb_d + (1.0 - lb_d) * jax.nn.sigmoid(zf)
        k = (1.0 - lb_d) * jax.nn.sigmoid(-zf)
        return jnp.log(f), k

    logf_f, k_f = gates(zf_fwd, lb[0])
    logf_b, k_b = gates(zf_bwd, lb[1])

    def flip(t):
        return jnp.flip(t, axis=1)

    o = (hgrn2_scan(q, logf_f, k_f, v)
         + flip(hgrn2_scan(flip(q), flip(logf_b), flip(k_b), flip(v))))
    o = rmsnorm(o, norm_g).reshape(B, S, HG_WIDTH)
    return o.astype(zg.dtype) * jax.nn.silu(zg)


def _linear_combine(e1, e2):
    a1, b1 = e1
    a2, b2 = e2
    return a1 * a2, a2 * b1 + b2


def rglru_branch(zx, zg, conv_w, conv_b, wa, ba, wx, bx, lam):
    B, S, W = zx.shape
    left = CONV_W // 2
    xp = jnp.pad(zx, ((0, 0), (left, CONV_W - 1 - left), (0, 0)))
    u = conv_b
    for j in range(CONV_W):
        u = u + xp[:, j:j + S] * conv_w[j]
    u32 = u.astype(jnp.float32)
    ub = u32.reshape(B, S, LRU_BLOCKS, LRU_BLOCK)

    def direction(d, reverse):
        r = jax.nn.sigmoid(jnp.einsum('bsnc,nce->bsne', ub, wa[d].astype(jnp.float32)).reshape(B, S, W)
                           + ba[d].astype(jnp.float32))
        i = jax.nn.sigmoid(jnp.einsum('bsnc,nce->bsne', ub, wx[d].astype(jnp.float32)).reshape(B, S, W)
                           + bx[d].astype(jnp.float32))
        log_a = -LRU_C * jax.nn.softplus(-lam[d].astype(jnp.float32)) * r
        a = jnp.exp(log_a)
        b = jnp.sqrt(-jnp.expm1(2.0 * log_a)) * (i * u32)
        _, h = lax.associative_scan(_linear_combine, (a, b), axis=1, reverse=reverse)
        return h

    h = direction(0, False) + direction(1, True)
    return h.astype(zx.dtype) * jax.nn.silu(zg)


def rope_partial(t, pos):
    half = ROPE_DIM // 2
    inv = ROPE_THETA ** (-jnp.arange(half, dtype=jnp.float32) * 2.0 / ROPE_DIM)
    ang = pos.astype(jnp.float32)[:, None] * inv[None, :]
    cos = jnp.cos(ang)[None, :, None, :]
    sin = jnp.sin(ang)[None, :, None, :]
    t32 = t.astype(jnp.float32)
    t1, t2 = t32[..., :half], t32[..., half:ROPE_DIM]
    rot = jnp.concatenate([t1 * cos - t2 * sin, t2 * cos + t1 * sin], axis=-1)
    return jnp.concatenate([rot.astype(t.dtype), t[..., ROPE_DIM:]], axis=-1)


def diff_attn_branch(zq, zk, zv, zg, lq1, lk1, lq2, lk2, norm_g, lam_init):
    B, S, _ = zq.shape
    pos = jnp.arange(S)
    q = rope_partial(zq.reshape(B, S, 2 * DA_HEADS, DA_HEAD_QK), pos) * (DA_HEAD_QK ** -0.5)
    k = rope_partial(zk.reshape(B, S, 2 * DA_HEADS, DA_HEAD_QK), pos)
    v = zv.reshape(B, S, DA_HEADS, DA_HEAD_V)
    lam = (jnp.exp(jnp.sum(lq1.astype(jnp.float32) * lk1.astype(jnp.float32)))
           - jnp.exp(jnp.sum(lq2.astype(jnp.float32) * lk2.astype(jnp.float32))) + lam_init)
    nb = S // Q_BLOCK
    qb = q.reshape(B, nb, Q_BLOCK, 2 * DA_HEADS, DA_HEAD_QK).transpose(1, 0, 3, 2, 4)
    kt = k.transpose(0, 2, 1, 3)
    vt = v.transpose(0, 2, 1, 3).astype(jnp.float32)

    def block(qblk):
        s = jnp.einsum('bhqd,bhkd->bhqk', qblk, kt).astype(jnp.float32)
        p = jax.nn.softmax(s, axis=-1).reshape(B, DA_HEADS, 2, Q_BLOCK, S)
        w = p[:, :, 0] - lam * p[:, :, 1]
        return jnp.einsum('bhqk,bhkv->bhqv', w, vt)

    o = lax.map(block, qb)
    o = o.transpose(1, 0, 3, 2, 4).reshape(B, S, DA_HEADS, DA_HEAD_V)
    o = rmsnorm(o, norm_g) * (1.0 - lam_init)
    return o.reshape(B, S, DA_WIDTH).astype(zg.dtype) * jax.nn.silu(zg)


def trunk(x, p, norm_g, w_in, w_out, hg_lb, hg_norm, lru_conv_w, lru_conv_b, lru_wa, lru_ba,
          lru_wx, lru_bx, lru_lam, da_lq1, da_lk1, da_lq2, da_lk2, da_norm, ple_w, ple_gate_w,
          final_norm):
    lb_all = jnp.cumsum(jax.nn.softmax(hg_lb.astype(jnp.float32), axis=0), axis=0)
    lb_all = lb_all - lb_all[:1]
    split_at = np.cumsum(SPLIT_SIZES)[:-1].tolist()
    h = x
    for l in range(DEPTH):
        hn = rmsnorm(h, norm_g[l])
        z = hn @ w_in[l]
        (hq, hf_f, hf_b, hi, hg, lx, lg, dq, dk, dv, dg) = jnp.split(z, split_at, axis=-1)
        o_hg = hgrn2_branch(hq, hf_f, hf_b, hi, hg, lb_all[l], hg_norm[l])
        o_lru = rglru_branch(lx, lg, lru_conv_w[l], lru_conv_b[l], lru_wa[l], lru_ba[l],
                             lru_wx[l], lru_bx[l], lru_lam[l])
        lam_init = 0.8 - 0.6 * math.exp(-0.3 * l)
        o_da = diff_attn_branch(dq, dk, dv, dg, da_lq1[l], da_lk1[l], da_lq2[l], da_lk2[l],
                                da_norm[l], lam_init)
        o = jnp.concatenate([o_hg, o_lru, o_da], axis=-1)
        h = h + o @ w_out[l]
        gate = jax.nn.sigmoid(h @ ple_gate_w[l])
        h = h + gate * (p[l] @ ple_w[l])
    return rmsnorm(h, final_norm)


def setup_inputs(seed: int = 0) -> dict:
    key = jax.random.key(seed)
    ks = jax.random.split(key, 24)
    f32 = jnp.float32

    def nrm(k, shape, scale):
        return jax.random.normal(k, shape, dtype=f32) * scale

    a0 = jax.random.uniform(ks[15], (DEPTH, 2, LRU_WIDTH), dtype=f32, minval=0.9, maxval=0.999)
    s0 = a0 ** (1.0 / LRU_C)
    return {
        'x_prompt': nrm(ks[0], (BATCH, SEQ, D_MODEL), 1.0),
        'x_sample': nrm(ks[1], (DEC_BATCH, DEC_SEQ, D_MODEL), 1.0),
        'p_prompt': nrm(ks[2], (DEPTH, BATCH, SEQ, PLE_DIM), 1.0),
        'p_sample': nrm(ks[3], (DEPTH, DEC_BATCH, DEC_SEQ, PLE_DIM), 1.0),
        'norm_g': 1.0 + nrm(ks[4], (DEPTH, D_MODEL), 0.05),
        'w_in': nrm(ks[5], (DEPTH, D_MODEL, D_IN), D_MODEL ** -0.5),
        'w_out': nrm(ks[6], (DEPTH, D_MIX, D_MODEL), D_MIX ** -0.5),
        'hg_lb': nrm(ks[7], (DEPTH, 2, HG_HEADS * HG_DK), 0.5),
        'hg_norm': 1.0 + nrm(ks[8], (DEPTH, HG_DV), 0.05),
        'lru_conv_w': nrm(ks[9], (DEPTH, CONV_W, LRU_WIDTH), CONV_W ** -0.5),
        'lru_conv_b': nrm(ks[10], (DEPTH, LRU_WIDTH), 0.01),
        'lru_wa': nrm(ks[11], (DEPTH, 2, LRU_BLOCKS, LRU_BLOCK, LRU_BLOCK), LRU_BLOCK ** -0.5),
        'lru_ba': nrm(ks[12], (DEPTH, 2, LRU_WIDTH), 0.01),
        'lru_wx': nrm(ks[13], (DEPTH, 2, LRU_BLOCKS, LRU_BLOCK, LRU_BLOCK), LRU_BLOCK ** -0.5),
        'lru_bx': nrm(ks[14], (DEPTH, 2, LRU_WIDTH), 0.01),
        'lru_lam': jnp.log(s0) - jnp.log1p(-s0),
        'da_lq1': nrm(ks[16], (DEPTH, DA_HEAD_QK), 0.1),
        'da_lk1': nrm(ks[17], (DEPTH, DA_HEAD_QK), 0.1),
        'da_lq2': nrm(ks[18], (DEPTH, DA_HEAD_QK), 0.1),
        'da_lk2': nrm(ks[19], (DEPTH, DA_HEAD_QK), 0.1),
        'da_norm': 1.0 + nrm(ks[20], (DEPTH, DA_HEAD_V), 0.05),
        'ple_w': nrm(ks[21], (DEPTH, PLE_DIM, D_MODEL), PLE_DIM ** -0.5),
        'ple_gate_w': nrm(ks[22], (DEPTH, D_MODEL, D_MODEL), D_MODEL ** -0.5),
        'final_norm': 1.0 + nrm(ks[23], (D_MODEL,), 0.05),
    }


def reference(x_prompt, x_sample, p_prompt, p_sample, norm_g, w_in, w_out, hg_lb, hg_norm,
              lru_conv_w, lru_conv_b, lru_wa, lru_ba, lru_wx, lru_bx, lru_lam, da_lq1, da_lk1,
              da_lq2, da_lk2, da_norm, ple_w, ple_gate_w, final_norm):
    y_prompt = trunk(x_prompt, p_prompt, norm_g, w_in, w_out, hg_lb, hg_norm, lru_conv_w,
                     lru_conv_b, lru_wa, lru_ba, lru_wx, lru_bx, lru_lam, da_lq1, da_lk1,
                     da_lq2, da_lk2, da_norm, ple_w, ple_gate_w, final_norm)
    y_sample = trunk(x_sample, p_sample, norm_g, w_in, w_out, hg_lb, hg_norm, lru_conv_w,
                     lru_conv_b, lru_wa, lru_ba, lru_wx, lru_bx, lru_lam, da_lq1, da_lk1,
                     da_lq2, da_lk2, da_norm, ple_w, ple_gate_w, final_norm)
    return (y_prompt, y_sample)
```

```python
import functools
import math

import numpy as np
import jax
import jax.numpy as jnp
from jax import lax
from jax.experimental import pallas as pl
from jax.experimental.pallas import tpu as pltpu

F32 = jnp.float32
BF16 = jnp.bfloat16

D_MODEL = 1024
PLE_DIM = 256
HEADS = 4
HG_W = 256
LRU_W = 512
LRU_BLOCK = 64
LRU_C = 8.0
DA_W = 256
DA_QK = 32
DA_V = 64
ROPE_THETA = 500000.0
ROPE_DIM = 8
RMS_EPS = 1e-6
LOG2E = 1.4426950408889634
D_HG_IN = 5 * HG_W
D_LRU_IN = 2 * LRU_W
D_DA_IN = 4 * DA_W

LANES = 128
VMEM_LIMIT = 56 * 1024 * 1024

TM_IN = 512
TM_OUT = 512
T_HG = 256
HG_CHUNK = 64
HG_BLK = 16
T_LRU = 256
TQ = 256
TK = 256


def _sigmoid(x):
    return 1.0 / (1.0 + jnp.exp(-x))


def _cparams(sem):
    return pltpu.CompilerParams(dimension_semantics=sem, vmem_limit_bytes=VMEM_LIMIT)


def _rope(t, cos, s1, s2):
    halves = []
    for j in range(2):
        th = t[:, LANES * j:LANES * (j + 1)]
        up = pltpu.roll(th, LANES - ROPE_DIM // 2, 1)
        dn = pltpu.roll(th, ROPE_DIM // 2, 1)
        halves.append(th * cos + up * s1 + dn * s2)
    return jnp.concatenate(halves, axis=1)


def _inproj_kernel(x_ref, g_ref, whg_ref, wlru_ref, wda_ref, lb_ref, cos_ref, s1_ref, s2_ref,
                   hq_ref, hkf_ref, hkb_ref, hv_ref, hg_ref, lff_ref, lfb_ref, lx_ref, lg_ref,
                   dqT_ref, dk_ref, dvT_ref, dg_ref):
    x = x_ref[...]
    ms = jnp.mean(x * x, axis=-1, keepdims=True)
    hn = (x * lax.rsqrt(ms + RMS_EPS) * g_ref[...]).astype(BF16)

    z = jnp.dot(hn, whg_ref[...], preferred_element_type=F32)
    zq = z[:, 0:HG_W]
    hq_ref[...] = (zq * _sigmoid(zq)).astype(BF16)
    for d, lf_ref, k_ref in ((0, lff_ref, hkf_ref), (1, lfb_ref, hkb_ref)):
        zf = z[:, HG_W * (1 + d):HG_W * (2 + d)]
        lb = lb_ref[d:d + 1, :]
        s = _sigmoid(zf)
        lf_ref[...] = jnp.log(lb + (1.0 - lb) * s)
        k_ref[...] = ((1.0 - lb) * (1.0 - s)).astype(BF16)
    hv_ref[...] = z[:, 3 * HG_W:4 * HG_W].astype(BF16)
    hg_ref[...] = z[:, 4 * HG_W:5 * HG_W].astype(BF16)

    z = jnp.dot(hn, wlru_ref[...], preferred_element_type=F32)
    lx_ref[...] = z[:, :LRU_W].astype(BF16)
    lg_ref[...] = z[:, LRU_W:].astype(BF16)

    z = jnp.dot(hn, wda_ref[...], preferred_element_type=F32)
    cos, s1, s2 = cos_ref[...], s1_ref[...], s2_ref[...]
    q = _rope(z[:, 0:DA_W], cos, s1, s2) * (DA_QK ** -0.5 * LOG2E)
    k = _rope(z[:, DA_W:2 * DA_W], cos, s1, s2)
    dqT_ref[...] = q.T.astype(BF16)
    dk_ref[...] = k.astype(BF16)
    dvT_ref[...] = z[:, 2 * DA_W:3 * DA_W].T.astype(BF16)
    dg_ref[...] = z[:, 3 * DA_W:4 * DA_W].astype(BF16)


def _inproj(x2d, B, S, g, whg, wlru, wda, lb, cos_t, s1_t, s2_t):
    R = B * S
    tm = TM_IN
    nt = S // tm
    row = lambda w: pl.BlockSpec((tm, w), lambda i: (i, 0))
    full = lambda a: pl.BlockSpec(a.shape, lambda i: (0,) * a.ndim)
    tab = pl.BlockSpec((tm, LANES), lambda i: (i % nt, 0))
    tr = pl.BlockSpec((None, DA_W, tm), lambda i: (i // nt, 0, i % nt))
    sds = jax.ShapeDtypeStruct
    out_shape = (
        sds((R, HG_W), BF16), sds((R, HG_W), BF16), sds((R, HG_W), BF16), sds((R, HG_W), BF16), sds((R, HG_W), BF16),
        sds((R, HG_W), F32), sds((R, HG_W), F32),
        sds((R, LRU_W), BF16), sds((R, LRU_W), BF16),
        sds((B, DA_W, S), BF16), sds((R, DA_W), BF16), sds((B, DA_W, S), BF16), sds((R, DA_W), BF16),
    )
    out_specs = (
        row(HG_W), row(HG_W), row(HG_W), row(HG_W), row(HG_W), row(HG_W), row(HG_W),
        row(LRU_W), row(LRU_W),
        tr, row(DA_W), tr, row(DA_W),
    )
    return pl.pallas_call(
        _inproj_kernel,
        grid=(R // tm,),
        in_specs=[row(D_MODEL), full(g), full(whg), full(wlru), full(wda), full(lb), tab, tab, tab],
        out_specs=out_specs,
        out_shape=out_shape,
        compiler_params=_cparams(("parallel",)),
        name="inproj",
    )(x2d, g, whg, wlru, wda, lb, cos_t, s1_t, s2_t)


def _split3(x):
    hi = x.astype(BF16)
    r1 = x - hi.astype(F32)
    mid = r1.astype(BF16)
    lo = (r1 - mid.astype(F32)).astype(BF16)
    return hi, mid, lo


def _dot_t(a, b):
    return lax.dot_general(a, b, (((1,), (1,)), ((), ())), preferred_element_type=F32)


def _dot_a(a, b):
    return lax.dot_general(a, b, (((0,), (0,)), ((), ())), preferred_element_type=F32)


def _head_rows(x_bf, hm_ref):
    zeros = jnp.zeros((HG_BLK, LANES), BF16)
    rows = []
    for h in range(HEADS):
        half = x_bf[:, LANES * (h // 2):LANES * (h // 2 + 1)] * hm_ref[h % 2]
        rows.append(jnp.concatenate([half, zeros] if h < 2 else [zeros, half], axis=1))
    return jnp.concatenate(rows, axis=0)


def _hgrn_tile(q_ref, lf_ref, k_ref, v_ref, ltri_ref, esel_ref, gmask_ref, hm_ref, bd_ref, s_ref, o_ref, reverse):
    T = T_HG
    nblk = HG_CHUNK // HG_BLK
    q = q_ref[...].astype(F32)
    k = k_ref[...].astype(F32)
    v_bf = v_ref[...]
    lf = lf_ref[...]

    ltri = ltri_ref[...]
    c = sum(jnp.dot(ltri, piece, preferred_element_type=F32) for piece in _split3(lf))

    nb = T // HG_BLK
    c3 = c.reshape(nb, HG_BLK, HG_W)
    k3 = k.reshape(nb, HG_BLK, HG_W)
    q3 = q.reshape(nb, HG_BLK, HG_W)
    dacc = jnp.zeros((T, HG_W), F32)
    for jl in range(HG_BLK):
        cj = c3[:, jl:jl + 1, :]
        kj = k3[:, jl:jl + 1, :]
        xj = (q3 * kj * jnp.exp(jnp.minimum(c3 - cj, 0.0))).reshape(T, HG_W).astype(BF16)
        dacc = dacc + jnp.dot(xj, esel_ref[jl], preferred_element_type=F32)
    dacc = dacc * gmask_ref[...]

    s_t = s_ref[...]
    bd = bd_ref[...]
    chunks = range(T // HG_CHUNK)
    for n in (reversed(chunks) if reverse else chunks):
        r0 = n * HG_CHUNK
        cn = c[r0:r0 + HG_CHUNK]
        qn = q[r0:r0 + HG_CHUNK]
        kn = k[r0:r0 + HG_CHUNK]
        vn = v_bf[r0:r0 + HG_CHUNK]
        ctot = cn[0:1] if reverse else cn[HG_CHUNK - 1:HG_CHUNK]
        qt = (qn * jnp.exp(cn)).astype(BF16)
        kt = (kn * jnp.exp(ctot - cn)).astype(BF16)
        o_inter = _dot_t(qt, s_t.astype(BF16))

        vbd = jnp.concatenate([_head_rows(vn[HG_BLK * b:HG_BLK * (b + 1)], hm_ref) for b in range(nblk)], axis=0)
        zpad64 = jnp.zeros((HG_CHUNK, HG_W), BF16)
        zcol = jnp.zeros((HG_BLK, LANES), F32)
        a_rows = []
        for i in range(nblk):
            rows = slice(HG_BLK * i, HG_BLK * (i + 1))
            d_i = dacc[r0 + HG_BLK * i:r0 + HG_BLK * (i + 1)]
            others = list(range(i + 1, nblk)) if reverse else list(range(i))
            if not others:
                a_rows.append(d_i)
                continue
            ref_row = cn[HG_BLK * (i + 1):HG_BLK * (i + 1) + 1] if reverse else cn[HG_BLK * i - 1:HG_BLK * i]
            qh = (qn[rows] * jnp.exp(cn[rows] - ref_row)).astype(BF16)
            pieces = []
            for b in others:
                brow = slice(HG_BLK * b, HG_BLK * (b + 1))
                kh = (kn[brow] * jnp.exp(ref_row - cn[brow])).astype(BF16)
                pieces.append(_head_rows(kh, hm_ref))
            if len(pieces) % 2:
                pieces = ([zpad64] + pieces) if reverse else (pieces + [zpad64])
            r_i = _dot_t(qh, jnp.concatenate(pieces, axis=0))
            if r_i.shape[1] == LANES:
                r_i = jnp.concatenate([zcol, r_i] if reverse else [r_i, zcol], axis=1)
            a_rows.append(r_i + d_i)
        a_t = jnp.concatenate(a_rows, axis=0).astype(BF16)
        o_ref[r0:r0 + HG_CHUNK, :] = o_inter + jnp.dot(a_t, vbd, preferred_element_type=F32)

        s_t = s_t * jnp.exp(ctot) + bd * _dot_a(vn, kt)
    s_ref[...] = s_t


def _hgrn_kernel(qf_ref, lff_ref, kf_ref, vf_ref, qb_ref, lfb_ref, kb_ref, vb_ref,
                 ltf_ref, ltb_ref, esel_ref, gmf_ref, gmb_ref, hm_ref, bd_ref,
                 of_ref, ob_ref, sf_ref, sb_ref):
    @pl.when(pl.program_id(1) == 0)
    def _():
        sf_ref[...] = jnp.zeros_like(sf_ref)
        sb_ref[...] = jnp.zeros_like(sb_ref)

    _hgrn_tile(qf_ref, lff_ref, kf_ref, vf_ref, ltf_ref, esel_ref, gmf_ref, hm_ref, bd_ref, sf_ref, of_ref, False)
    _hgrn_tile(qb_ref, lfb_ref, kb_ref, vb_ref, ltb_ref, esel_ref, gmb_ref, hm_ref, bd_ref, sb_ref, ob_ref, True)


@functools.lru_cache(maxsize=None)
def _hgrn_consts():
    T = T_HG
    t = np.arange(T)
    same_chunk = (t[:, None] // HG_CHUNK) == (t[None, :] // HG_CHUNK)
    ltf = (same_chunk & (t[None, :] <= t[:, None])).astype(np.float32)
    ltb = (same_chunk & (t[None, :] >= t[:, None])).astype(np.float32)
    lane = np.arange(HG_W)
    col = np.arange(HG_W)
    esel = np.zeros((HG_BLK, HG_W, HG_W), np.float32)
    for jl in range(HG_BLK):
        esel[jl] = ((lane[:, None] // 64) == ((col[None, :] % 64) // HG_BLK)) & ((col[None, :] % HG_BLK) == jl)
    blk_of_row = (t % HG_CHUNK) // HG_BLK
    own = (col[None, :] // 64) == blk_of_row[:, None]
    jl_col = col[None, :] % HG_BLK
    gmf = (own & (jl_col <= (t % HG_BLK)[:, None])).astype(np.float32)
    gmb = (own & (jl_col >= (t % HG_BLK)[:, None])).astype(np.float32)
    hm = np.stack([(np.arange(LANES) < 64), (np.arange(LANES) >= 64)]).astype(np.float32)
    hm = np.broadcast_to(hm[:, None, :], (2, HG_BLK, LANES)).copy()
    bd = ((lane[:, None] // 64) == (lane[None, :] // 64)).astype(np.float32)
    return ltf, ltb, esel, gmf, gmb, hm, bd


def _hgrn(hq, lff, hkf, hv, lfb, hkb, B, S):
    T = T_HG
    nt = S // T
    ltf, ltb, esel, gmf, gmb, hm, bd = _hgrn_consts()
    consts = (jnp.asarray(ltf, BF16), jnp.asarray(ltb, BF16), jnp.asarray(esel, BF16),
              jnp.asarray(gmf, F32), jnp.asarray(gmb, F32), jnp.asarray(hm, BF16), jnp.asarray(bd, F32))
    r3 = lambda a: a.reshape(B, S, HG_W)
    fwd = pl.BlockSpec((None, T, HG_W), lambda b, i: (b, i, 0))
    bwd = pl.BlockSpec((None, T, HG_W), lambda b, i: (b, nt - 1 - i, 0))
    full = lambda a: pl.BlockSpec(a.shape, lambda b, i: (0,) * a.ndim)
    of, ob = pl.pallas_call(
        _hgrn_kernel,
        grid=(B, nt),
        in_specs=[fwd, fwd, fwd, fwd, bwd, bwd, bwd, bwd] + [full(a) for a in consts],
        out_specs=(fwd, bwd),
        out_shape=(jax.ShapeDtypeStruct((B, S, HG_W), F32), jax.ShapeDtypeStruct((B, S, HG_W), F32)),
        scratch_shapes=[pltpu.VMEM((HG_W, HG_W), F32), pltpu.VMEM((HG_W, HG_W), F32)],
        compiler_params=_cparams(("parallel", "arbitrary")),
        name="hgrn",
    )(r3(hq), r3(lff), r3(hkf), r3(hv), r3(hq), r3(lfb), r3(hkb), r3(hv), *consts)
    return of.reshape(B * S, HG_W), ob.reshape(B * S, HG_W)


LRU_HALF = LRU_W // 2


def _scan_chunk(a, b, carry, reverse):
    T, C = a.shape
    nt = T // 8
    av = a.reshape(nt, 8, C)
    bv = b.reshape(nt, 8, C)
    rid = lax.broadcasted_iota(jnp.int32, (nt, 8, C), 1)
    for d in (1, 2, 4):
        if reverse:
            m, sh = rid < 8 - d, 8 - d
        else:
            m, sh = rid >= d, d
        ad = pltpu.roll(av, sh, 1)
        bd = pltpu.roll(bv, sh, 1)
        bv = bv + av * jnp.where(m, bd, 0.0)
        av = av * jnp.where(m, ad, 1.0)
    hs = [None] * nt
    for i in (range(nt - 1, -1, -1) if reverse else range(nt)):
        h = av[i] * carry + bv[i]
        hs[i] = h
        carry = h[0:1] if reverse else h[7:8]
    return jnp.concatenate(hs, axis=0), carry


def _lru_kernel(lx_ref, lg_ref, cw_ref, cb_ref, w_ref, bias_ref, cneg_ref, out_ref, hf_ref, *, S):
    T = T_LRU
    n = S // T
    C = LRU_HALF
    cw = cw_ref[...]
    cb = cb_ref[...]

    def conv_u(i):
        t0 = pl.multiple_of(i * T, T)
        cur = lx_ref[pl.ds(t0, T), :].astype(F32)
        ps = pl.multiple_of(jnp.maximum(t0 - 16, 0), 16)
        prev = jnp.where(i > 0, lx_ref[pl.ds(ps, 16), :].astype(F32), 0.0)
        ns = pl.multiple_of(jnp.minimum(t0 + T, S - 16), 16)
        nxt = jnp.where(i < n - 1, lx_ref[pl.ds(ns, 16), :].astype(F32), 0.0)
        ext = jnp.concatenate([prev, cur, nxt], axis=0)
        return (cb + ext[14:14 + T] * cw[0:1] + ext[15:15 + T] * cw[1:2] + cur * cw[2:3]
                + ext[17:17 + T] * cw[3:4])

    def gates(d, u):
        pre = jnp.dot(u.astype(BF16), w_ref[:, 2 * C * d:2 * C * (d + 1)], preferred_element_type=F32)
        r = _sigmoid(pre[:, :C] + bias_ref[2 * d:2 * d + 1, :])
        ig = _sigmoid(pre[:, C:] + bias_ref[2 * d + 1:2 * d + 2, :])
        log_a = cneg_ref[d:d + 1, :] * r
        a = jnp.exp(log_a)
        t = jnp.tanh(log_a)
        b = jnp.sqrt(-2.0 * t / (1.0 - t)) * (ig * u)
        return a, b

    def fwd_body(i, carry):
        a, b = gates(0, conv_u(i))
        h, carry = _scan_chunk(a, b, carry, False)
        hf_ref[pl.ds(pl.multiple_of(i * T, T), T), :] = h
        return carry

    lax.fori_loop(0, n, fwd_body, jnp.zeros((1, C), F32))

    def bwd_body(ii, carry):
        i = n - 1 - ii
        a, b = gates(1, conv_u(i))
        h, carry = _scan_chunk(a, b, carry, True)
        rows = pl.ds(pl.multiple_of(i * T, T), T)
        g = lg_ref[rows, :].astype(F32)
        out_ref[rows, :] = ((hf_ref[rows, :] + h) * (g * _sigmoid(g))).astype(BF16)
        return carry

    lax.fori_loop(0, n, bwd_body, jnp.zeros((1, C), F32))


def _lru(lx, lg, B, S, cw, cb, w, bias, cneg):
    C = LRU_HALF
    seq = pl.BlockSpec((None, S, C), lambda b, c: (b, 0, c))
    vec = lambda r: pl.BlockSpec((r, C), lambda b, c: (0, c))
    out = pl.pallas_call(
        functools.partial(_lru_kernel, S=S),
        grid=(B, LRU_W // C),
        in_specs=[seq, seq, vec(4), vec(1), pl.BlockSpec((None, C, 4 * C), lambda b, c: (c, 0, 0)), vec(4), vec(2)],
        out_specs=seq,
        out_shape=jax.ShapeDtypeStruct((B, S, LRU_W), BF16),
        scratch_shapes=[pltpu.VMEM((S, C), F32)],
        compiler_params=_cparams(("parallel", "parallel")),
        name="lru",
    )(lx.reshape(B, S, LRU_W), lg.reshape(B, S, LRU_W), cw, cb, w, bias, cneg)
    return out.reshape(B * S, LRU_W)


PV_ROWS = DA_V + 16
M_INIT = -1e30


def _attn_kernel(lam_ref, qT_ref, k_ref, vT_ref, dg_ref, ng_ref, out_ref, rhs_ref, m_ref, acc_ref, *, S, lam_init):
    nk = S // TK
    qT = qT_ref[...].astype(F32)
    head_of_row = lax.broadcasted_iota(jnp.int32, (DA_W, TQ), 0) // DA_QK
    for hc in range(2 * HEADS):
        rhs_ref[hc] = jnp.where(head_of_row == hc, qT, 0.0).astype(BF16)
    m_ref[...] = jnp.full(m_ref.shape, M_INIT, F32)
    acc_ref[...] = jnp.zeros_like(acc_ref)
    ones = jnp.ones((PV_ROWS - DA_V, TK), BF16)

    def body(j, _):
        k0 = pl.multiple_of(j * TK, TK)
        k = k_ref[pl.ds(k0, TK), :]
        vT = vT_ref[:, pl.ds(k0, TK)]
        for hc in range(2 * HEADS):
            h = hc // 2
            s = jnp.dot(k, rhs_ref[hc], preferred_element_type=F32)
            m_old = m_ref[hc]
            m_new = jnp.maximum(m_old, jnp.max(s, axis=0, keepdims=True))
            p = jnp.exp2(s - m_new).astype(BF16)
            lhs = jnp.concatenate([vT[DA_V * h:DA_V * (h + 1), :], ones], axis=0)
            acc_ref[hc] = jnp.exp2(m_old - m_new) * acc_ref[hc] + jnp.dot(lhs, p, preferred_element_type=F32)
            m_ref[hc] = m_new
        return 0

    lax.fori_loop(0, nk, body, 0)

    lam = lam_ref[0]
    outs = []
    for h in range(HEADS):
        a0 = acc_ref[2 * h]
        a1 = acc_ref[2 * h + 1]
        o = a0[:DA_V] * (1.0 / a0[DA_V:DA_V + 1]) - lam * (a1[:DA_V] * (1.0 / a1[DA_V:DA_V + 1]))
        ms = jnp.mean(o * o, axis=0, keepdims=True)
        outs.append(o * lax.rsqrt(ms + RMS_EPS))
    o = jnp.concatenate(outs, axis=0).T
    g = dg_ref[...].astype(F32)
    out_ref[...] = (o * ng_ref[...] * (1.0 - lam_init) * (g * _sigmoid(g))).astype(BF16)


def _attn(lam, dqT, dk, dvT, dg, ng, B, S, lam_init):
    nq = S // TQ
    out = pl.pallas_call(
        functools.partial(_attn_kernel, S=S, lam_init=lam_init),
        grid=(B, nq),
        in_specs=[
            pl.BlockSpec(memory_space=pltpu.SMEM),
            pl.BlockSpec((None, DA_W, TQ), lambda b, i: (b, 0, i)),
            pl.BlockSpec((None, S, DA_W), lambda b, i: (b, 0, 0)),
            pl.BlockSpec((None, DA_W, S), lambda b, i: (b, 0, 0)),
            pl.BlockSpec((None, TQ, DA_W), lambda b, i: (b, i, 0)),
            pl.BlockSpec((1, DA_W), lambda b, i: (0, 0)),
        ],
        out_specs=pl.BlockSpec((None, TQ, DA_W), lambda b, i: (b, i, 0)),
        out_shape=jax.ShapeDtypeStruct((B, S, DA_W), BF16),
        scratch_shapes=[
            pltpu.VMEM((2 * HEADS, DA_W, TQ), BF16),
            pltpu.VMEM((2 * HEADS, 1, TQ), F32),
            pltpu.VMEM((2 * HEADS, PV_ROWS, TQ), F32),
        ],
        compiler_params=_cparams(("parallel", "parallel")),
        name="attn",
    )(lam, dqT, dk.reshape(B, S, DA_W), dvT, dg.reshape(B, S, DA_W), ng)
    return out.reshape(B * S, DA_W)


def _outproj_kernel(h_ref, of_ref, ob_ref, hg_ref, olru_ref, oda_ref, p_ref, hgn_ref, ones_ref,
                    wout_ref, gw_ref, pw_ref, fn_ref, out_ref, *, final):
    o = of_ref[...] + ob_ref[...]
    sq = o * o
    hi = sq.astype(BF16)
    lo = (sq - hi.astype(F32)).astype(BF16)
    ones = ones_ref[...]
    ssum = jnp.dot(hi, ones, preferred_element_type=F32) + jnp.dot(lo, ones, preferred_element_type=F32)
    g = hg_ref[...].astype(F32)
    ohg = (o * lax.rsqrt(ssum * (1.0 / 64.0) + RMS_EPS) * hgn_ref[...] * (g * _sigmoid(g))).astype(BF16)

    acc = h_ref[...]
    acc = acc + jnp.dot(ohg, wout_ref[0:HG_W, :], preferred_element_type=F32)
    acc = acc + jnp.dot(olru_ref[...], wout_ref[HG_W:HG_W + LRU_W, :], preferred_element_type=F32)
    acc = acc + jnp.dot(oda_ref[...], wout_ref[HG_W + LRU_W:, :], preferred_element_type=F32)
    gate = _sigmoid(jnp.dot(acc.astype(BF16), gw_ref[...], preferred_element_type=F32))
    h2 = acc + gate * jnp.dot(p_ref[...].astype(BF16), pw_ref[...], preferred_element_type=F32)
    if final:
        ms = jnp.mean(h2 * h2, axis=-1, keepdims=True)
        h2 = h2 * lax.rsqrt(ms + RMS_EPS) * fn_ref[...]
    out_ref[...] = h2


def _outproj(h, of, ob, hg, olru, oda, p, hgn, ones, wout, gw, pw, fn, final):
    R = h.shape[0]
    tm = TM_OUT
    row = lambda w: pl.BlockSpec((tm, w), lambda i: (i, 0))
    full = lambda a: pl.BlockSpec(a.shape, lambda i: (0,) * a.ndim)
    return pl.pallas_call(
        functools.partial(_outproj_kernel, final=final),
        grid=(R // tm,),
        in_specs=[row(D_MODEL), row(HG_W), row(HG_W), row(HG_W), row(LRU_W), row(DA_W), row(PLE_DIM),
                  full(hgn), full(ones), full(wout), full(gw), full(pw), full(fn)],
        out_specs=row(D_MODEL),
        out_shape=jax.ShapeDtypeStruct((R, D_MODEL), F32),
        compiler_params=_cparams(("parallel",)),
        name="outproj",
    )(h, of, ob, hg, olru, oda, p, hgn, ones, wout, gw, pw, fn)


def _rope_tables(S):
    half = ROPE_DIM // 2
    inv = ROPE_THETA ** (-jnp.arange(half, dtype=F32) * 2.0 / ROPE_DIM)
    ang = jnp.arange(S, dtype=F32)[:, None] * inv[None, :]
    cos, sin = jnp.cos(ang), jnp.sin(ang)
    ones = jnp.ones((S, DA_QK - ROPE_DIM), F32)
    zeros = jnp.zeros((S, DA_QK - ROPE_DIM), F32)
    z4 = jnp.zeros((S, half), F32)
    rep = lambda t: jnp.tile(t, (1, LANES // DA_QK))
    return (rep(jnp.concatenate([cos, cos, ones], axis=1)),
            rep(jnp.concatenate([-sin, z4, zeros], axis=1)),
            rep(jnp.concatenate([z4, sin, zeros], axis=1)))


def _lru_gate_weights(wa, wx):
    per_half = LRU_HALF // LRU_BLOCK
    halves = []
    for c in range(LRU_W // LRU_HALF):
        tiles = []
        for d in range(2):
            for w in (wa, wx):
                blocks = [w[d, per_half * c + m] for m in range(per_half)]
                tiles.append(jax.scipy.linalg.block_diag(*blocks))
        halves.append(jnp.concatenate(tiles, axis=1))
    return jnp.stack(halves).astype(BF16)


def _trunk(x, p, B, S, prm):
    R = B * S
    h = x.reshape(R, D_MODEL)
    depth = prm["w_in"].shape[0]
    cos_t, s1_t, s2_t = _rope_tables(S)
    lb_all = jnp.cumsum(jax.nn.softmax(prm["hg_lb"].astype(F32), axis=0), axis=0)
    lb_all = lb_all - lb_all[:1]
    lane = np.arange(HG_W)
    ones_bd = jnp.asarray((lane[:, None] // 64) == (lane[None, :] // 64), BF16)
    for l in range(depth):
        w_in = prm["w_in"][l].astype(BF16)
        (hq, hkf, hkb, hv, hg, lff, lfb, lx, lg, dqT, dk, dvT, dg) = _inproj(
            h, B, S, prm["norm_g"][l][None, :], w_in[:, :D_HG_IN], w_in[:, D_HG_IN:D_HG_IN + D_LRU_IN],
            w_in[:, D_HG_IN + D_LRU_IN:], lb_all[l], cos_t, s1_t, s2_t)

        of, ob = _hgrn(hq, lff, hkf, hv, lfb, hkb, B, S)

        cneg = -LRU_C * jax.nn.softplus(-prm["lru_lam"][l].astype(F32))
        bias = jnp.stack([prm["lru_ba"][l, 0], prm["lru_bx"][l, 0], prm["lru_ba"][l, 1], prm["lru_bx"][l, 1]])
        olru = _lru(lx, lg, B, S, prm["lru_conv_w"][l], prm["lru_conv_b"][l][None, :],
                    _lru_gate_weights(prm["lru_wa"][l], prm["lru_wx"][l]), bias, cneg)

        lam_init = 0.8 - 0.6 * math.exp(-0.3 * l)
        lam = (jnp.exp(jnp.sum(prm["da_lq1"][l].astype(F32) * prm["da_lk1"][l].astype(F32)))
               - jnp.exp(jnp.sum(prm["da_lq2"][l].astype(F32) * prm["da_lk2"][l].astype(F32))) + lam_init)
        oda = _attn(lam.reshape(1), dqT, dk, dvT, dg, jnp.tile(prm["da_norm"][l], HEADS)[None, :], B, S, lam_init)

        h = _outproj(h, of, ob, hg, olru, oda, p[l].reshape(R, PLE_DIM), jnp.tile(prm["hg_norm"][l], HEADS)[None, :],
                     ones_bd, prm["w_out"][l].astype(BF16), prm["ple_gate_w"][l].astype(BF16),
                     prm["ple_w"][l].astype(BF16), prm["final_norm"][None, :], final=(l == depth - 1))
    return h.reshape(B, S, D_MODEL)


def kernel(x_prompt, x_sample, p_prompt, p_sample, norm_g, w_in, w_out, hg_lb, hg_norm, lru_conv_w, lru_conv_b,
           lru_wa, lru_ba, lru_wx, lru_bx, lru_lam, da_lq1, da_lk1, da_lq2, da_lk2, da_norm, ple_w, ple_gate_w,
           final_norm):
    prm = dict(norm_g=norm_g, w_in=w_in, w_out=w_out, hg_lb=hg_lb, hg_norm=hg_norm, lru_conv_w=lru_conv_w,
               lru_conv_b=lru_conv_b, lru_wa=lru_wa, lru_ba=lru_ba, lru_wx=lru_wx, lru_bx=lru_bx, lru_lam=lru_lam,
               da_lq1=da_lq1, da_lk1=da_lk1, da_lq2=da_lq2, da_lk2=da_lk2, da_norm=da_norm, ple_w=ple_w,
               ple_gate_w=ple_gate_w, final_norm=final_norm)
    outs = []
    for x, p in ((x_prompt, p_prompt), (x_sample, p_sample)):
        B, S, _ = x.shape
        outs.append(_trunk(x, p, B, S, prm))
    return tuple(outs)
```

```python
import functools
import math

import numpy as np
import jax
import jax.numpy as jnp
from jax import lax
from jax.experimental import pallas as pl
from jax.experimental.pallas import tpu as pltpu

F32 = jnp.float32
BF16 = jnp.bfloat16

D_MODEL = 1024
PLE_DIM = 256
HEADS = 4
HG_W = 256
LRU_W = 512
LRU_BLOCK = 64
LRU_C = 8.0
DA_W = 256
DA_QK = 32
DA_V = 64
ROPE_THETA = 500000.0
ROPE_DIM = 8
RMS_EPS = 1e-6
LOG2E = 1.4426950408889634
D_HG_IN = 5 * HG_W
D_LRU_IN = 2 * LRU_W
D_DA_IN = 4 * DA_W

LANES = 128
VMEM_LIMIT = 56 * 1024 * 1024

TM_IN = 1024
TM_OUT = 1024
T_HG = 256
HG_CHUNK = 64
HG_BLK = 16
T_LRU = 256
TQ = 256
TK = 256


def _sigmoid(x):
    return 1.0 / (1.0 + jnp.exp2(x * -LOG2E))


def _cparams(sem):
    return pltpu.CompilerParams(dimension_semantics=sem, vmem_limit_bytes=VMEM_LIMIT)


def _rope(t, cos, s1, s2):
    halves = []
    for j in range(2):
        th = t[:, LANES * j:LANES * (j + 1)]
        up = pltpu.roll(th, LANES - ROPE_DIM // 2, 1)
        dn = pltpu.roll(th, ROPE_DIM // 2, 1)
        halves.append(th * cos + up * s1 + dn * s2)
    return jnp.concatenate(halves, axis=1)


def _inproj_kernel(x_ref, g_ref, whg_ref, wlru_ref, wda_ref, lb_ref, cos_ref, s1_ref, s2_ref,
                   hq_ref, hkf_ref, hkb_ref, hv_ref, hg_ref, lff_ref, lfb_ref, lx_ref, lg_ref,
                   dqT_ref, dk_ref, dvT_ref, dg_ref):
    x = x_ref[...]
    ms = jnp.mean(x * x, axis=-1, keepdims=True)
    hn = (x * lax.rsqrt(ms + RMS_EPS) * g_ref[...]).astype(BF16)

    z = jnp.dot(hn, whg_ref[...], preferred_element_type=F32)
    zq = z[:, 0:HG_W]
    hq_ref[...] = (zq * _sigmoid(zq)).astype(BF16)
    for d, lf_ref, k_ref in ((0, lff_ref, hkf_ref), (1, lfb_ref, hkb_ref)):
        zf = z[:, HG_W * (1 + d):HG_W * (2 + d)]
        lb = lb_ref[d:d + 1, :]
        s = _sigmoid(zf)
        lf_ref[...] = jnp.log2(lb + (1.0 - lb) * s)
        k_ref[...] = ((1.0 - lb) * (1.0 - s)).astype(BF16)
    hv_ref[...] = z[:, 3 * HG_W:4 * HG_W].astype(BF16)
    hg_ref[...] = z[:, 4 * HG_W:5 * HG_W].astype(BF16)

    z = jnp.dot(hn, wlru_ref[...], preferred_element_type=F32)
    lx_ref[...] = z[:, :LRU_W].astype(BF16)
    lg_ref[...] = z[:, LRU_W:].astype(BF16)

    z = jnp.dot(hn, wda_ref[...], preferred_element_type=F32)
    cos, s1, s2 = cos_ref[...], s1_ref[...], s2_ref[...]
    q = _rope(z[:, 0:DA_W], cos, s1, s2) * (DA_QK ** -0.5 * LOG2E)
    k = _rope(z[:, DA_W:2 * DA_W], cos, s1, s2)
    dqT_ref[...] = q.T.astype(BF16)
    dk_ref[...] = k.astype(BF16)
    dvT_ref[...] = z[:, 2 * DA_W:3 * DA_W].T.astype(BF16)
    dg_ref[...] = z[:, 3 * DA_W:4 * DA_W].astype(BF16)


def _inproj(x2d, B, S, g, whg, wlru, wda, lb, cos_t, s1_t, s2_t):
    R = B * S
    tm = min(TM_IN, S)
    nt = S // tm
    row = lambda w: pl.BlockSpec((tm, w), lambda i: (i, 0))
    full = lambda a: pl.BlockSpec(a.shape, lambda i: (0,) * a.ndim, pipeline_mode=pl.Buffered(1))
    tab = pl.BlockSpec((tm, LANES), lambda i: (i % nt, 0))
    tr = pl.BlockSpec((None, DA_W, tm), lambda i: (i // nt, 0, i % nt))
    sds = jax.ShapeDtypeStruct
    out_shape = (
        sds((R, HG_W), BF16), sds((R, HG_W), BF16), sds((R, HG_W), BF16), sds((R, HG_W), BF16), sds((R, HG_W), BF16),
        sds((R, HG_W), F32), sds((R, HG_W), F32),
        sds((R, LRU_W), BF16), sds((R, LRU_W), BF16),
        sds((B, DA_W, S), BF16), sds((R, DA_W), BF16), sds((B, DA_W, S), BF16), sds((R, DA_W), BF16),
    )
    out_specs = (
        row(HG_W), row(HG_W), row(HG_W), row(HG_W), row(HG_W), row(HG_W), row(HG_W),
        row(LRU_W), row(LRU_W),
        tr, row(DA_W), tr, row(DA_W),
    )
    return pl.pallas_call(
        _inproj_kernel,
        grid=(R // tm,),
        in_specs=[row(D_MODEL), full(g), full(whg), full(wlru), full(wda), full(lb), tab, tab, tab],
        out_specs=out_specs,
        out_shape=out_shape,
        compiler_params=_cparams(("parallel",)),
        name="inproj",
    )(x2d, g, whg, wlru, wda, lb, cos_t, s1_t, s2_t)


def _split2(x):
    hi = x.astype(BF16)
    lo = (x - hi.astype(F32)).astype(BF16)
    return hi, lo


def _dot_t(a, b):
    return lax.dot_general(a, b, (((1,), (1,)), ((), ())), preferred_element_type=F32)


def _dot_a(a, b):
    return lax.dot_general(a, b, (((0,), (0,)), ((), ())), preferred_element_type=F32)


def _head_rows(x_bf, hm_ref):
    zeros = jnp.zeros((HG_BLK, LANES), BF16)
    rows = []
    for h in range(HEADS):
        half = x_bf[:, LANES * (h // 2):LANES * (h // 2 + 1)] * hm_ref[h % 2]
        rows.append(jnp.concatenate([half, zeros] if h < 2 else [zeros, half], axis=1))
    return jnp.concatenate(rows, axis=0)


def _hgrn_tile(q_ref, lf_ref, k_ref, v_ref, ltri_ref, esel_ref, gmask_ref, hm_ref, bd_ref, s_ref, o_ref, reverse):
    T = T_HG
    nblk = HG_CHUNK // HG_BLK
    q = q_ref[...].astype(F32)
    k = k_ref[...].astype(F32)
    v_bf = v_ref[...]
    lf = lf_ref[...]

    ltri = ltri_ref[...]
    c = sum(jnp.dot(ltri, piece, preferred_element_type=F32) for piece in _split2(lf))

    nb = T // HG_BLK
    half = HG_BLK // 2
    c4 = c.reshape(nb, 2, half, HG_W)
    ck4 = (c - jnp.log2(k)).reshape(nb, 2, half, HG_W)
    q4 = q.reshape(nb, 2, half, HG_W)
    far = 0 if reverse else 1
    dacc = jnp.zeros((T, HG_W), F32)
    dfar = jnp.zeros((T // 2, HG_W), F32)
    for jl in range(HG_BLK):
        cj = ck4[:, jl // half, jl % half:jl % half + 1, :][:, None]
        if jl // half == far:
            xj = q4[:, far:far + 1] * jnp.exp2(jnp.minimum(c4[:, far:far + 1] - cj, 0.0))
            dfar = dfar + jnp.dot(xj.reshape(T // 2, HG_W).astype(BF16), esel_ref[jl], preferred_element_type=F32)
        else:
            xj = q4 * jnp.exp2(jnp.minimum(c4 - cj, 0.0))
            dacc = dacc + jnp.dot(xj.reshape(T, HG_W).astype(BF16), esel_ref[jl], preferred_element_type=F32)
    d4 = dacc.reshape(nb, 2, half, HG_W)
    dfar4 = d4[:, far:far + 1] + dfar.reshape(nb, 1, half, HG_W)
    d4 = jnp.concatenate([d4[:, 0:1], dfar4] if far else [dfar4, d4[:, 1:2]], axis=1)
    dacc = d4.reshape(T, HG_W) * gmask_ref[...]

    yield

    bd = bd_ref[...]
    zpad64 = jnp.zeros((HG_CHUNK, HG_W), BF16)
    zcol = jnp.zeros((HG_BLK, LANES), F32)
    chunks = list(range(T // HG_CHUNK))
    per_chunk = {}
    for n in chunks:
        r0 = n * HG_CHUNK
        cn = c[r0:r0 + HG_CHUNK]
        qn = q[r0:r0 + HG_CHUNK]
        kn = k[r0:r0 + HG_CHUNK]
        vn = v_bf[r0:r0 + HG_CHUNK]
        ctot = cn[0:1] if reverse else cn[HG_CHUNK - 1:HG_CHUNK]
        qt = (qn * jnp.exp2(cn)).astype(BF16)
        kt = (kn * jnp.exp2(ctot - cn)).astype(BF16)
        upd = bd * _dot_a(vn, kt)
        vbd = jnp.concatenate([_head_rows(vn[HG_BLK * b:HG_BLK * (b + 1)], hm_ref) for b in range(nblk)], axis=0)
        a_rows = []
        for i in range(nblk):
            rows = slice(HG_BLK * i, HG_BLK * (i + 1))
            d_i = dacc[r0 + HG_BLK * i:r0 + HG_BLK * (i + 1)]
            others = list(range(i + 1, nblk)) if reverse else list(range(i))
            if not others:
                a_rows.append(d_i)
                continue
            ref_row = cn[HG_BLK * (i + 1):HG_BLK * (i + 1) + 1] if reverse else cn[HG_BLK * i - 1:HG_BLK * i]
            qh = (qn[rows] * jnp.exp2(cn[rows] - ref_row)).astype(BF16)
            pieces = []
            for b in others:
                brow = slice(HG_BLK * b, HG_BLK * (b + 1))
                kh = (kn[brow] * jnp.exp2(ref_row - cn[brow])).astype(BF16)
                pieces.append(_head_rows(kh, hm_ref))
            if len(pieces) % 2:
                pieces = ([zpad64] + pieces) if reverse else (pieces + [zpad64])
            r_i = _dot_t(qh, jnp.concatenate(pieces, axis=0))
            if r_i.shape[1] == LANES:
                r_i = jnp.concatenate([zcol, r_i] if reverse else [r_i, zcol], axis=1)
            a_rows.append(r_i + d_i)
        a_t = jnp.concatenate(a_rows, axis=0).astype(BF16)
        per_chunk[n] = (qt, jnp.exp2(ctot), upd, a_t, vbd)
    yield

    o_intra = {n: jnp.dot(per_chunk[n][3], per_chunk[n][4], preferred_element_type=F32) for n in chunks}
    s_t = s_ref[...]
    for n in (reversed(chunks) if reverse else chunks):
        qt, decay, upd, _, _ = per_chunk[n]
        r0 = n * HG_CHUNK
        o_ref[r0:r0 + HG_CHUNK, :] = _dot_t(qt, s_t.astype(BF16)) + o_intra[n]
        s_t = s_t * decay + upd
    s_ref[...] = s_t
    yield


def _hgrn_kernel(qf_ref, lff_ref, kf_ref, vf_ref, qb_ref, lfb_ref, kb_ref, vb_ref,
                 ltf_ref, ltb_ref, esel_ref, gmf_ref, gmb_ref, hm_ref, bd_ref,
                 of_ref, ob_ref, sf_ref, sb_ref):
    @pl.when(pl.program_id(1) == 0)
    def _():
        sf_ref[...] = jnp.zeros_like(sf_ref)
        sb_ref[...] = jnp.zeros_like(sb_ref)

    tiles = [
        _hgrn_tile(qf_ref, lff_ref, kf_ref, vf_ref, ltf_ref, esel_ref, gmf_ref, hm_ref, bd_ref, sf_ref, of_ref, False),
        _hgrn_tile(qb_ref, lfb_ref, kb_ref, vb_ref, ltb_ref, esel_ref, gmb_ref, hm_ref, bd_ref, sb_ref, ob_ref, True),
    ]
    for _ in range(3):
        for t in tiles:
            next(t)


@functools.lru_cache(maxsize=None)
def _hgrn_consts():
    T = T_HG
    t = np.arange(T)
    same_chunk = (t[:, None] // HG_CHUNK) == (t[None, :] // HG_CHUNK)
    ltf = (same_chunk & (t[None, :] <= t[:, None])).astype(np.float32)
    ltb = (same_chunk & (t[None, :] >= t[:, None])).astype(np.float32)
    lane = np.arange(HG_W)
    col = np.arange(HG_W)
    esel = np.zeros((HG_BLK, HG_W, HG_W), np.float32)
    for jl in range(HG_BLK):
        esel[jl] = ((lane[:, None] // 64) == ((col[None, :] % 64) // HG_BLK)) & ((col[None, :] % HG_BLK) == jl)
    blk_of_row = (t % HG_CHUNK) // HG_BLK
    own = (col[None, :] // 64) == blk_of_row[:, None]
    jl_col = col[None, :] % HG_BLK
    gmf = (own & (jl_col <= (t % HG_BLK)[:, None])).astype(np.float32)
    gmb = (own & (jl_col >= (t % HG_BLK)[:, None])).astype(np.float32)
    hm = np.stack([(np.arange(LANES) < 64), (np.arange(LANES) >= 64)]).astype(np.float32)
    hm = np.broadcast_to(hm[:, None, :], (2, HG_BLK, LANES)).copy()
    bd = ((lane[:, None] // 64) == (lane[None, :] // 64)).astype(np.float32)
    return ltf, ltb, esel, gmf, gmb, hm, bd


def _hgrn(hq, lff, hkf, hv, lfb, hkb, B, S):
    T = T_HG
    nt = S // T
    ltf, ltb, esel, gmf, gmb, hm, bd = _hgrn_consts()
    consts = (jnp.asarray(ltf, BF16), jnp.asarray(ltb, BF16), jnp.asarray(esel, BF16),
              jnp.asarray(gmf, F32), jnp.asarray(gmb, F32), jnp.asarray(hm, BF16), jnp.asarray(bd, F32))
    r3 = lambda a: a.reshape(B, S, HG_W)
    fwd = pl.BlockSpec((None, T, HG_W), lambda b, i: (b, i, 0))
    bwd = pl.BlockSpec((None, T, HG_W), lambda b, i: (b, nt - 1 - i, 0))
    full = lambda a: pl.BlockSpec(a.shape, lambda b, i: (0,) * a.ndim, pipeline_mode=pl.Buffered(1))
    of, ob = pl.pallas_call(
        _hgrn_kernel,
        grid=(B, nt),
        in_specs=[fwd, fwd, fwd, fwd, bwd, bwd, bwd, bwd] + [full(a) for a in consts],
        out_specs=(fwd, bwd),
        out_shape=(jax.ShapeDtypeStruct((B, S, HG_W), F32), jax.ShapeDtypeStruct((B, S, HG_W), F32)),
        scratch_shapes=[pltpu.VMEM((HG_W, HG_W), F32), pltpu.VMEM((HG_W, HG_W), F32)],
        compiler_params=_cparams(("parallel", "arbitrary")),
        name="hgrn",
    )(r3(hq), r3(lff), r3(hkf), r3(hv), r3(hq), r3(lfb), r3(hkb), r3(hv), *consts)
    return of.reshape(B * S, HG_W), ob.reshape(B * S, HG_W)


LRU_HALF = LRU_W // 2
LRU_HALO = 16


def _scan_chunk(a, b, carry, reverse):
    T, C = a.shape
    nt = T // 8
    av = a.reshape(nt, 8, C)
    bv = b.reshape(nt, 8, C)
    rid = lax.broadcasted_iota(jnp.int32, (nt, 8, C), 1)
    for d in (1, 2, 4):
        if reverse:
            m, sh = rid < 8 - d, 8 - d
        else:
            m, sh = rid >= d, d
        ad = pltpu.roll(av, sh, 1)
        bd = pltpu.roll(bv, sh, 1)
        bv = bv + av * jnp.where(m, bd, 0.0)
        av = av * jnp.where(m, ad, 1.0)
    hs = [None] * nt
    for i in (range(nt - 1, -1, -1) if reverse else range(nt)):
        h = av[i] * carry + bv[i]
        hs[i] = h
        carry = h[0:1] if reverse else h[7:8]
    return jnp.concatenate(hs, axis=0), carry


def _lru_kernel(lx_ref, lg_ref, cw_ref, cb_ref, w_ref, bias_ref, cneg_ref, out_ref, u_ref, hf_ref, hb_ref, *, S):
    T = T_LRU
    n = S // T
    C = LRU_HALF
    cw = cw_ref[...]
    cb = cb_ref[...]

    def conv_body(i, _):
        t0 = pl.multiple_of(i * T, T)
        cur = lx_ref[pl.ds(t0, T), :].astype(F32)
        ps = pl.multiple_of(jnp.maximum(t0 - LRU_HALO, 0), LRU_HALO)
        prev = jnp.where(i > 0, lx_ref[pl.ds(ps, LRU_HALO), :].astype(F32), 0.0)
        ns = pl.multiple_of(jnp.minimum(t0 + T, S - LRU_HALO), LRU_HALO)
        nxt = jnp.where(i < n - 1, lx_ref[pl.ds(ns, LRU_HALO), :].astype(F32), 0.0)
        ext = jnp.concatenate([prev, cur, nxt], axis=0)
        h = LRU_HALO
        u_ref[pl.ds(t0, T), :] = (cb + ext[h - 2:h - 2 + T] * cw[0:1] + ext[h - 1:h - 1 + T] * cw[1:2]
                                  + cur * cw[2:3] + ext[h + 1:h + 1 + T] * cw[3:4])
        return 0

    lax.fori_loop(0, n, conv_body, 0)

    def conv_u(i):
        return u_ref[pl.ds(pl.multiple_of(i * T, T), T), :]

    def gates(d, u):
        pre = jnp.dot(u.astype(BF16), w_ref[:, 2 * C * d:2 * C * (d + 1)], preferred_element_type=F32)
        r = _sigmoid(pre[:, :C] + bias_ref[2 * d:2 * d + 1, :])
        ig = _sigmoid(pre[:, C:] + bias_ref[2 * d + 1:2 * d + 2, :])
        log_a = cneg_ref[d:d + 1, :] * r
        a = jnp.exp(log_a)
        y = 1.0 - a * a
        b = jnp.where(y > 0.0, y * lax.rsqrt(y), 0.0) * (ig * u)
        return a, b

    def emit(rows, h):
        g = lg_ref[rows, :].astype(F32)
        out_ref[rows, :] = (h * (g * _sigmoid(g))).astype(BF16)

    def step(i, carry, meet):
        cf, cr = carry
        j = n - 1 - i
        rows_i = pl.ds(pl.multiple_of(i * T, T), T)
        rows_j = pl.ds(pl.multiple_of(j * T, T), T)
        af, bf = gates(0, conv_u(i))
        ar, br = gates(1, conv_u(j))
        hf, cf = _scan_chunk(af, bf, cf, False)
        hr, cr = _scan_chunk(ar, br, cr, True)
        if meet:
            emit(rows_i, hf + hb_ref[rows_i, :])
            emit(rows_j, hf_ref[rows_j, :] + hr)
        else:
            hf_ref[rows_i, :] = hf
            hb_ref[rows_j, :] = hr
        return cf, cr

    zero = jnp.zeros((1, C), F32)
    carry = lax.fori_loop(0, n // 2, functools.partial(step, meet=False), (zero, zero))
    lax.fori_loop(n // 2, n, functools.partial(step, meet=True), carry)


def _lru(lx, lg, B, S, cw, cb, w, bias, cneg):
    C = LRU_HALF
    seq = pl.BlockSpec((None, S, C), lambda b, c: (b, 0, c))
    vec = lambda r: pl.BlockSpec((r, C), lambda b, c: (0, c))
    out = pl.pallas_call(
        functools.partial(_lru_kernel, S=S),
        grid=(B, LRU_W // C),
        in_specs=[seq, seq, vec(4), vec(1), pl.BlockSpec((None, C, 4 * C), lambda b, c: (c, 0, 0)), vec(4), vec(2)],
        out_specs=seq,
        out_shape=jax.ShapeDtypeStruct((B, S, LRU_W), BF16),
        scratch_shapes=[pltpu.VMEM((S, C), F32), pltpu.VMEM((S, C), F32), pltpu.VMEM((S, C), F32)],
        compiler_params=_cparams(("parallel", "parallel")),
        name="lru",
    )(lx.reshape(B, S, LRU_W), lg.reshape(B, S, LRU_W), cw, cb, w, bias, cneg)
    return out.reshape(B * S, LRU_W)


PV_ROWS = DA_V + 16
M_INIT = -1e30


def _attn_kernel(lam_ref, qT_ref, k_ref, vT_ref, dg_ref, ng_ref, out_ref, rhs_ref, m_ref, acc_ref, sa_ref, sb_ref,
                 *, S, lam_init):
    nk = S // TK
    qT = qT_ref[...].astype(F32)
    head_of_row = lax.broadcasted_iota(jnp.int32, (DA_W, TQ), 0) // DA_QK
    for hc in range(2 * HEADS):
        rhs_ref[hc] = jnp.where(head_of_row == hc, qT, 0.0).astype(BF16)
    m_ref[...] = jnp.full(m_ref.shape, M_INIT, F32)
    acc_ref[...] = jnp.zeros_like(acc_ref)
    ones = jnp.ones((PV_ROWS - DA_V, TK), BF16)

    def scores(j, s_ref):
        k = k_ref[pl.ds(pl.multiple_of(j * TK, TK), TK), :]
        for hc in range(2 * HEADS):
            s_ref[hc] = jnp.dot(k, rhs_ref[hc], preferred_element_type=F32)

    def softmax_pv(j, s_ref):
        vT = vT_ref[:, pl.ds(pl.multiple_of(j * TK, TK), TK)]
        for hc in range(2 * HEADS):
            h = hc // 2
            s = s_ref[hc]
            m_old = m_ref[hc]
            m_new = jnp.maximum(m_old, jnp.max(s, axis=0, keepdims=True))
            p = jnp.exp2(s - m_new).astype(BF16)
            lhs = jnp.concatenate([vT[DA_V * h:DA_V * (h + 1), :], ones], axis=0)
            acc_ref[hc] = jnp.exp2(m_old - m_new) * acc_ref[hc] + jnp.dot(lhs, p, preferred_element_type=F32)
            m_ref[hc] = m_new

    scores(0, sa_ref)

    def body(i, _):
        scores(2 * i + 1, sb_ref)
        softmax_pv(2 * i, sa_ref)
        scores(2 * i + 2, sa_ref)
        softmax_pv(2 * i + 1, sb_ref)
        return 0

    lax.fori_loop(0, nk // 2 - 1, body, 0)
    scores(nk - 1, sb_ref)
    softmax_pv(nk - 2, sa_ref)
    softmax_pv(nk - 1, sb_ref)

    lam = lam_ref[0]
    outs = []
    for h in range(HEADS):
        a0 = acc_ref[2 * h]
        a1 = acc_ref[2 * h + 1]
        o = a0[:DA_V] * (1.0 / a0[DA_V:DA_V + 1]) - lam * (a1[:DA_V] * (1.0 / a1[DA_V:DA_V + 1]))
        ms = jnp.mean(o * o, axis=0, keepdims=True)
        outs.append(o * lax.rsqrt(ms + RMS_EPS))
    o = jnp.concatenate(outs, axis=0).T
    g = dg_ref[...].astype(F32)
    out_ref[...] = (o * ng_ref[...] * (1.0 - lam_init) * (g * _sigmoid(g))).astype(BF16)


def _attn(lam, dqT, dk, dvT, dg, ng, B, S, lam_init):
    nq = S // TQ
    out = pl.pallas_call(
        functools.partial(_attn_kernel, S=S, lam_init=lam_init),
        grid=(B, nq),
        in_specs=[
            pl.BlockSpec(memory_space=pltpu.SMEM),
            pl.BlockSpec((None, DA_W, TQ), lambda b, i: (b, 0, i)),
            pl.BlockSpec((None, S, DA_W), lambda b, i: (b, 0, 0)),
            pl.BlockSpec((None, DA_W, S), lambda b, i: (b, 0, 0)),
            pl.BlockSpec((None, TQ, DA_W), lambda b, i: (b, i, 0)),
            pl.BlockSpec((1, DA_W), lambda b, i: (0, 0)),
        ],
        out_specs=pl.BlockSpec((None, TQ, DA_W), lambda b, i: (b, i, 0)),
        out_shape=jax.ShapeDtypeStruct((B, S, DA_W), BF16),
        scratch_shapes=[
            pltpu.VMEM((2 * HEADS, DA_W, TQ), BF16),
            pltpu.VMEM((2 * HEADS, 1, TQ), F32),
            pltpu.VMEM((2 * HEADS, PV_ROWS, TQ), F32),
            pltpu.VMEM((2 * HEADS, TK, TQ), F32),
            pltpu.VMEM((2 * HEADS, TK, TQ), F32),
        ],
        compiler_params=_cparams(("parallel", "parallel")),
        name="attn",
    )(lam, dqT, dk.reshape(B, S, DA_W), dvT, dg.reshape(B, S, DA_W), ng)
    return out.reshape(B * S, DA_W)


def _outproj_kernel(h_ref, of_ref, ob_ref, hg_ref, olru_ref, oda_ref, p_ref, hgn_ref, ones_ref,
                    wout_ref, gw_ref, pw_ref, fn_ref, out_ref, *, final):
    o = of_ref[...] + ob_ref[...]
    sq = o * o
    hi = sq.astype(BF16)
    lo = (sq - hi.astype(F32)).astype(BF16)
    ones = ones_ref[...]
    ssum = jnp.dot(hi, ones, preferred_element_type=F32) + jnp.dot(lo, ones, preferred_element_type=F32)
    g = hg_ref[...].astype(F32)
    ohg = (o * lax.rsqrt(ssum * (1.0 / 64.0) + RMS_EPS) * hgn_ref[...] * (g * _sigmoid(g))).astype(BF16)

    acc = h_ref[...]
    acc = acc + jnp.dot(ohg, wout_ref[0:HG_W, :], preferred_element_type=F32)
    acc = acc + jnp.dot(olru_ref[...], wout_ref[HG_W:HG_W + LRU_W, :], preferred_element_type=F32)
    acc = acc + jnp.dot(oda_ref[...], wout_ref[HG_W + LRU_W:, :], preferred_element_type=F32)
    gate = _sigmoid(jnp.dot(acc.astype(BF16), gw_ref[...], preferred_element_type=F32))
    h2 = acc + gate * jnp.dot(p_ref[...].astype(BF16), pw_ref[...], preferred_element_type=F32)
    if final:
        ms = jnp.mean(h2 * h2, axis=-1, keepdims=True)
        h2 = h2 * lax.rsqrt(ms + RMS_EPS) * fn_ref[...]
    out_ref[...] = h2


def _outproj(h, of, ob, hg, olru, oda, p, hgn, ones, wout, gw, pw, fn, final):
    R = h.shape[0]
    tm = min(TM_OUT, R)
    row = lambda w: pl.BlockSpec((tm, w), lambda i: (i, 0))
    full = lambda a: pl.BlockSpec(a.shape, lambda i: (0,) * a.ndim, pipeline_mode=pl.Buffered(1))
    return pl.pallas_call(
        functools.partial(_outproj_kernel, final=final),
        grid=(R // tm,),
        in_specs=[row(D_MODEL), row(HG_W), row(HG_W), row(HG_W), row(LRU_W), row(DA_W), row(PLE_DIM),
                  full(hgn), full(ones), full(wout), full(gw), full(pw), full(fn)],
        out_specs=row(D_MODEL),
        out_shape=jax.ShapeDtypeStruct((R, D_MODEL), F32),
        compiler_params=_cparams(("parallel",)),
        name="outproj",
    )(h, of, ob, hg, olru, oda, p, hgn, ones, wout, gw, pw, fn)


def _rope_tables(S):
    half = ROPE_DIM // 2
    inv = ROPE_THETA ** (-jnp.arange(half, dtype=F32) * 2.0 / ROPE_DIM)
    ang = jnp.arange(S, dtype=F32)[:, None] * inv[None, :]
    cos, sin = jnp.cos(ang), jnp.sin(ang)
    ones = jnp.ones((S, DA_QK - ROPE_DIM), F32)
    zeros = jnp.zeros((S, DA_QK - ROPE_DIM), F32)
    z4 = jnp.zeros((S, half), F32)
    rep = lambda t: jnp.tile(t, (1, LANES // DA_QK))
    return (rep(jnp.concatenate([cos, cos, ones], axis=1)),
            rep(jnp.concatenate([-sin, z4, zeros], axis=1)),
            rep(jnp.concatenate([z4, sin, zeros], axis=1)))


def _lru_gate_weights(wa, wx):
    per_half = LRU_HALF // LRU_BLOCK
    halves = []
    for c in range(LRU_W // LRU_HALF):
        tiles = []
        for d in range(2):
            for w in (wa, wx):
                blocks = [w[d, per_half * c + m] for m in range(per_half)]
                tiles.append(jax.scipy.linalg.block_diag(*blocks))
        halves.append(jnp.concatenate(tiles, axis=1))
    return jnp.stack(halves).astype(BF16)


def _trunk(x, p, B, S, prm):
    R = B * S
    h = x.reshape(R, D_MODEL)
    depth = prm["w_in"].shape[0]
    cos_t, s1_t, s2_t = _rope_tables(S)
    lb_all = jnp.cumsum(jax.nn.softmax(prm["hg_lb"].astype(F32), axis=0), axis=0)
    lb_all = lb_all - lb_all[:1]
    lane = np.arange(HG_W)
    ones_bd = jnp.asarray((lane[:, None] // 64) == (lane[None, :] // 64), BF16)
    for l in range(depth):
        w_in = prm["w_in"][l].astype(BF16)
        (hq, hkf, hkb, hv, hg, lff, lfb, lx, lg, dqT, dk, dvT, dg) = _inproj(
            h, B, S, prm["norm_g"][l][None, :], w_in[:, :D_HG_IN], w_in[:, D_HG_IN:D_HG_IN + D_LRU_IN],
            w_in[:, D_HG_IN + D_LRU_IN:], lb_all[l], cos_t, s1_t, s2_t)

        of, ob = _hgrn(hq, lff, hkf, hv, lfb, hkb, B, S)

        cneg = -LRU_C * jax.nn.softplus(-prm["lru_lam"][l].astype(F32))
        bias = jnp.stack([prm["lru_ba"][l, 0], prm["lru_bx"][l, 0], prm["lru_ba"][l, 1], prm["lru_bx"][l, 1]])
        olru = _lru(lx, lg, B, S, prm["lru_conv_w"][l], prm["lru_conv_b"][l][None, :],
                    _lru_gate_weights(prm["lru_wa"][l], prm["lru_wx"][l]), bias, cneg)

        lam_init = 0.8 - 0.6 * math.exp(-0.3 * l)
        lam = (jnp.exp(jnp.sum(prm["da_lq1"][l].astype(F32) * prm["da_lk1"][l].astype(F32)))
               - jnp.exp(jnp.sum(prm["da_lq2"][l].astype(F32) * prm["da_lk2"][l].astype(F32))) + lam_init)
        oda = _attn(lam.reshape(1), dqT, dk, dvT, dg, jnp.tile(prm["da_norm"][l], HEADS)[None, :], B, S, lam_init)

        h = _outproj(h, of, ob, hg, olru, oda, p[l].reshape(R, PLE_DIM), jnp.tile(prm["hg_norm"][l], HEADS)[None, :],
                     ones_bd, prm["w_out"][l].astype(BF16), prm["ple_gate_w"][l].astype(BF16),
                     prm["ple_w"][l].astype(BF16), prm["final_norm"][None, :], final=(l == depth - 1))
    return h.reshape(B, S, D_MODEL)


def kernel(x_prompt, x_sample, p_prompt, p_sample, norm_g, w_in, w_out, hg_lb, hg_norm, lru_conv_w, lru_conv_b,
           lru_wa, lru_ba, lru_wx, lru_bx, lru_lam, da_lq1, da_lk1, da_lq2, da_lk2, da_norm, ple_w, ple_gate_w,
           final_norm):
    prm = dict(norm_g=norm_g, w_in=w_in, w_out=w_out, hg_lb=hg_lb, hg_norm=hg_norm, lru_conv_w=lru_conv_w,
               lru_conv_b=lru_conv_b, lru_wa=lru_wa, lru_ba=lru_ba, lru_wx=lru_wx, lru_bx=lru_bx, lru_lam=lru_lam,
               da_lq1=da_lq1, da_lk1=da_lk1, da_lq2=da_lq2, da_lk2=da_lk2, da_norm=da_norm, ple_w=ple_w,
               ple_gate_w=ple_gate_w, final_norm=final_norm)
    outs = []
    for x, p in ((x_prompt, p_prompt), (x_sample, p_sample)):
        B, S, _ = x.shape
        outs.append(_trunk(x, p, B, S, prm))
    return tuple(outs)
```

```python
import functools
import math

import numpy as np
import jax
import jax.numpy as jnp
from jax import lax
from jax.experimental import pallas as pl
from jax.experimental.pallas import tpu as pltpu

F32 = jnp.float32
BF16 = jnp.bfloat16

D_MODEL = 1024
PLE_DIM = 256
HEADS = 4
HG_W = 256
LRU_W = 512
LRU_BLOCK = 64
LRU_C = 8.0
DA_W = 256
DA_QK = 32
DA_V = 64
ROPE_THETA = 500000.0
ROPE_DIM = 8
RMS_EPS = 1e-6
LOG2E = 1.4426950408889634
D_HG_IN = 5 * HG_W
D_LRU_IN = 2 * LRU_W
D_DA_IN = 4 * DA_W

LANES = 128
VMEM_LIMIT = 56 * 1024 * 1024

TM_IN = 1024
TM_OUT = 1024
T_HG = 256
HG_CHUNK = 64
HG_BLK = 16
T_LRU = 256
TQ = 256
TK = 512


def _sigmoid(x):
    return 1.0 / (1.0 + jnp.exp2(x * -LOG2E))


def _cparams(sem):
    return pltpu.CompilerParams(dimension_semantics=sem, vmem_limit_bytes=VMEM_LIMIT)


def _rope(t, cos, s1, s2):
    halves = []
    for j in range(2):
        th = t[:, LANES * j:LANES * (j + 1)]
        up = pltpu.roll(th, LANES - ROPE_DIM // 2, 1)
        dn = pltpu.roll(th, ROPE_DIM // 2, 1)
        halves.append(th * cos + up * s1 + dn * s2)
    return jnp.concatenate(halves, axis=1)


def _inproj_kernel(x_ref, g_ref, whg_ref, wlru_ref, wda_ref, lb_ref, cos_ref, s1_ref, s2_ref,
                   hq_ref, hkf_ref, hkb_ref, hv_ref, hg_ref, lff_ref, lfb_ref, lx_ref, lg_ref,
                   dqT_ref, dk_ref, dvT_ref, dg_ref):
    x = x_ref[...]
    ms = jnp.mean(x * x, axis=-1, keepdims=True)
    hn = (x * lax.rsqrt(ms + RMS_EPS) * g_ref[...]).astype(BF16)

    z = jnp.dot(hn, whg_ref[...], preferred_element_type=F32)
    zq = z[:, 0:HG_W]
    hq_ref[...] = (zq * _sigmoid(zq)).astype(BF16)
    for d, lf_ref, k_ref in ((0, lff_ref, hkf_ref), (1, lfb_ref, hkb_ref)):
        zf = z[:, HG_W * (1 + d):HG_W * (2 + d)]
        lb = lb_ref[d:d + 1, :]
        s = _sigmoid(zf)
        lf_ref[...] = jnp.log2(lb + (1.0 - lb) * s)
        k_ref[...] = ((1.0 - lb) * (1.0 - s)).astype(BF16)
    hv_ref[...] = z[:, 3 * HG_W:4 * HG_W].astype(BF16)
    hg_ref[...] = z[:, 4 * HG_W:5 * HG_W].astype(BF16)

    z = jnp.dot(hn, wlru_ref[...], preferred_element_type=F32)
    lx_ref[...] = z[:, :LRU_W].astype(BF16)
    lg_ref[...] = z[:, LRU_W:].astype(BF16)

    z = jnp.dot(hn, wda_ref[...], preferred_element_type=F32)
    cos, s1, s2 = cos_ref[...], s1_ref[...], s2_ref[...]
    q = _rope(z[:, 0:DA_W], cos, s1, s2) * (DA_QK ** -0.5 * LOG2E)
    k = _rope(z[:, DA_W:2 * DA_W], cos, s1, s2)
    dqT_ref[...] = q.T.astype(BF16)
    dk_ref[...] = k.astype(BF16)
    dvT_ref[...] = z[:, 2 * DA_W:3 * DA_W].T.astype(BF16)
    dg_ref[...] = z[:, 3 * DA_W:4 * DA_W].astype(BF16)


def _inproj(x2d, B, S, g, whg, wlru, wda, lb, cos_t, s1_t, s2_t):
    R = B * S
    tm = min(TM_IN, S)
    nt = S // tm
    row = lambda w: pl.BlockSpec((tm, w), lambda i: (i, 0))
    full = lambda a: pl.BlockSpec(a.shape, lambda i: (0,) * a.ndim, pipeline_mode=pl.Buffered(1))
    tab = pl.BlockSpec((tm, LANES), lambda i: (i % nt, 0))
    tr = pl.BlockSpec((None, DA_W, tm), lambda i: (i // nt, 0, i % nt))
    sds = jax.ShapeDtypeStruct
    out_shape = (
        sds((R, HG_W), BF16), sds((R, HG_W), BF16), sds((R, HG_W), BF16), sds((R, HG_W), BF16), sds((R, HG_W), BF16),
        sds((R, HG_W), F32), sds((R, HG_W), F32),
        sds((R, LRU_W), BF16), sds((R, LRU_W), BF16),
        sds((B, DA_W, S), BF16), sds((R, DA_W), BF16), sds((B, DA_W, S), BF16), sds((R, DA_W), BF16),
    )
    out_specs = (
        row(HG_W), row(HG_W), row(HG_W), row(HG_W), row(HG_W), row(HG_W), row(HG_W),
        row(LRU_W), row(LRU_W),
        tr, row(DA_W), tr, row(DA_W),
    )
    return pl.pallas_call(
        _inproj_kernel,
        grid=(R // tm,),
        in_specs=[row(D_MODEL), full(g), full(whg), full(wlru), full(wda), full(lb), tab, tab, tab],
        out_specs=out_specs,
        out_shape=out_shape,
        compiler_params=_cparams(("parallel",)),
        name="inproj",
    )(x2d, g, whg, wlru, wda, lb, cos_t, s1_t, s2_t)


def _split2(x):
    hi = x.astype(BF16)
    lo = (x - hi.astype(F32)).astype(BF16)
    return hi, lo


def _dot_t(a, b):
    return lax.dot_general(a, b, (((1,), (1,)), ((), ())), preferred_element_type=F32)


def _dot_a(a, b):
    return lax.dot_general(a, b, (((0,), (0,)), ((), ())), preferred_element_type=F32)


def _head_rows(x_bf, hm_ref):
    zeros = jnp.zeros((HG_BLK, LANES), BF16)
    rows = []
    for h in range(HEADS):
        half = x_bf[:, LANES * (h // 2):LANES * (h // 2 + 1)] * hm_ref[h % 2]
        rows.append(jnp.concatenate([half, zeros] if h < 2 else [zeros, half], axis=1))
    return jnp.concatenate(rows, axis=0)


def _hgrn_tile(q_ref, lf_ref, k_ref, v_ref, ltri_ref, esel_ref, gmask_ref, hm_ref, bd_ref, s_ref, o_ref, reverse):
    T = T_HG
    nblk = HG_CHUNK // HG_BLK
    q = q_ref[...].astype(F32)
    k = k_ref[...].astype(F32)
    v_bf = v_ref[...]
    lf = lf_ref[...]

    ltri = ltri_ref[...]
    c = sum(jnp.dot(ltri, piece, preferred_element_type=F32) for piece in _split2(lf))

    nb = T // HG_BLK
    half = HG_BLK // 2
    c4 = c.reshape(nb, 2, half, HG_W)
    ck4 = (c - jnp.log2(k)).reshape(nb, 2, half, HG_W)
    q4 = q.reshape(nb, 2, half, HG_W)
    far = 0 if reverse else 1
    dacc = jnp.zeros((T, HG_W), F32)
    dfar = jnp.zeros((T // 2, HG_W), F32)
    for jl in range(HG_BLK):
        cj = ck4[:, jl // half, jl % half:jl % half + 1, :][:, None]
        if jl // half == far:
            xj = q4[:, far:far + 1] * jnp.exp2(jnp.minimum(c4[:, far:far + 1] - cj, 0.0))
            dfar = dfar + jnp.dot(xj.reshape(T // 2, HG_W).astype(BF16), esel_ref[jl], preferred_element_type=F32)
        else:
            xj = q4 * jnp.exp2(jnp.minimum(c4 - cj, 0.0))
            dacc = dacc + jnp.dot(xj.reshape(T, HG_W).astype(BF16), esel_ref[jl], preferred_element_type=F32)
    d4 = dacc.reshape(nb, 2, half, HG_W)
    dfar4 = d4[:, far:far + 1] + dfar.reshape(nb, 1, half, HG_W)
    d4 = jnp.concatenate([d4[:, 0:1], dfar4] if far else [dfar4, d4[:, 1:2]], axis=1)
    dacc = d4.reshape(T, HG_W) * gmask_ref[...]

    yield

    bd = bd_ref[...]
    zpad64 = jnp.zeros((HG_CHUNK, HG_W), BF16)
    zcol = jnp.zeros((HG_BLK, LANES), F32)
    chunks = list(range(T // HG_CHUNK))
    per_chunk = {}
    for n in chunks:
        r0 = n * HG_CHUNK
        cn = c[r0:r0 + HG_CHUNK]
        qn = q[r0:r0 + HG_CHUNK]
        kn = k[r0:r0 + HG_CHUNK]
        vn = v_bf[r0:r0 + HG_CHUNK]
        ctot = cn[0:1] if reverse else cn[HG_CHUNK - 1:HG_CHUNK]
        qt = (qn * jnp.exp2(cn)).astype(BF16)
        kt = (kn * jnp.exp2(ctot - cn)).astype(BF16)
        upd = bd * _dot_a(vn, kt)
        vbd = jnp.concatenate([_head_rows(vn[HG_BLK * b:HG_BLK * (b + 1)], hm_ref) for b in range(nblk)], axis=0)
        a_rows = []
        for i in range(nblk):
            rows = slice(HG_BLK * i, HG_BLK * (i + 1))
            d_i = dacc[r0 + HG_BLK * i:r0 + HG_BLK * (i + 1)]
            others = list(range(i + 1, nblk)) if reverse else list(range(i))
            if not others:
                a_rows.append(d_i)
                continue
            ref_row = cn[HG_BLK * (i + 1):HG_BLK * (i + 1) + 1] if reverse else cn[HG_BLK * i - 1:HG_BLK * i]
            qh = (qn[rows] * jnp.exp2(cn[rows] - ref_row)).astype(BF16)
            pieces = []
            for b in others:
                brow = slice(HG_BLK * b, HG_BLK * (b + 1))
                kh = (kn[brow] * jnp.exp2(ref_row - cn[brow])).astype(BF16)
                pieces.append(_head_rows(kh, hm_ref))
            if len(pieces) % 2:
                pieces = ([zpad64] + pieces) if reverse else (pieces + [zpad64])
            r_i = _dot_t(qh, jnp.concatenate(pieces, axis=0))
            if r_i.shape[1] == LANES:
                r_i = jnp.concatenate([zcol, r_i] if reverse else [r_i, zcol], axis=1)
            a_rows.append(r_i + d_i)
        a_t = jnp.concatenate(a_rows, axis=0).astype(BF16)
        per_chunk[n] = (qt, jnp.exp2(ctot), upd, a_t, vbd)
    yield

    o_intra = {n: jnp.dot(per_chunk[n][3], per_chunk[n][4], preferred_element_type=F32) for n in chunks}
    s_t = s_ref[...]
    for n in (reversed(chunks) if reverse else chunks):
        qt, decay, upd, _, _ = per_chunk[n]
        r0 = n * HG_CHUNK
        o_ref[r0:r0 + HG_CHUNK, :] = _dot_t(qt, s_t.astype(BF16)) + o_intra[n]
        s_t = s_t * decay + upd
    s_ref[...] = s_t
    yield


def _hgrn_kernel(qf_ref, lff_ref, kf_ref, vf_ref, qb_ref, lfb_ref, kb_ref, vb_ref,
                 ltf_ref, ltb_ref, esel_ref, gmf_ref, gmb_ref, hm_ref, bd_ref,
                 of_ref, ob_ref, sf_ref, sb_ref):
    @pl.when(pl.program_id(1) == 0)
    def _():
        sf_ref[...] = jnp.zeros_like(sf_ref)
        sb_ref[...] = jnp.zeros_like(sb_ref)

    tiles = [
        _hgrn_tile(qf_ref, lff_ref, kf_ref, vf_ref, ltf_ref, esel_ref, gmf_ref, hm_ref, bd_ref, sf_ref, of_ref, False),
        _hgrn_tile(qb_ref, lfb_ref, kb_ref, vb_ref, ltb_ref, esel_ref, gmb_ref, hm_ref, bd_ref, sb_ref, ob_ref, True),
    ]
    for _ in range(3):
        for t in tiles:
            next(t)


@functools.lru_cache(maxsize=None)
def _hgrn_consts():
    T = T_HG
    t = np.arange(T)
    same_chunk = (t[:, None] // HG_CHUNK) == (t[None, :] // HG_CHUNK)
    ltf = (same_chunk & (t[None, :] <= t[:, None])).astype(np.float32)
    ltb = (same_chunk & (t[None, :] >= t[:, None])).astype(np.float32)
    lane = np.arange(HG_W)
    col = np.arange(HG_W)
    esel = np.zeros((HG_BLK, HG_W, HG_W), np.float32)
    for jl in range(HG_BLK):
        esel[jl] = ((lane[:, None] // 64) == ((col[None, :] % 64) // HG_BLK)) & ((col[None, :] % HG_BLK) == jl)
    blk_of_row = (t % HG_CHUNK) // HG_BLK
    own = (col[None, :] // 64) == blk_of_row[:, None]
    jl_col = col[None, :] % HG_BLK
    gmf = (own & (jl_col <= (t % HG_BLK)[:, None])).astype(np.float32)
    gmb = (own & (jl_col >= (t % HG_BLK)[:, None])).astype(np.float32)
    hm = np.stack([(np.arange(LANES) < 64), (np.arange(LANES) >= 64)]).astype(np.float32)
    hm = np.broadcast_to(hm[:, None, :], (2, HG_BLK, LANES)).copy()
    bd = ((lane[:, None] // 64) == (lane[None, :] // 64)).astype(np.float32)
    return ltf, ltb, esel, gmf, gmb, hm, bd


def _hgrn(hq, lff, hkf, hv, lfb, hkb, B, S):
    T = T_HG
    nt = S // T
    ltf, ltb, esel, gmf, gmb, hm, bd = _hgrn_consts()
    consts = (jnp.asarray(ltf, BF16), jnp.asarray(ltb, BF16), jnp.asarray(esel, BF16),
              jnp.asarray(gmf, F32), jnp.asarray(gmb, F32), jnp.asarray(hm, BF16), jnp.asarray(bd, F32))
    r3 = lambda a: a.reshape(B, S, HG_W)
    fwd = pl.BlockSpec((None, T, HG_W), lambda b, i: (b, i, 0))
    bwd = pl.BlockSpec((None, T, HG_W), lambda b, i: (b, nt - 1 - i, 0))
    full = lambda a: pl.BlockSpec(a.shape, lambda b, i: (0,) * a.ndim, pipeline_mode=pl.Buffered(1))
    of, ob = pl.pallas_call(
        _hgrn_kernel,
        grid=(B, nt),
        in_specs=[fwd, fwd, fwd, fwd, bwd, bwd, bwd, bwd] + [full(a) for a in consts],
        out_specs=(fwd, bwd),
        out_shape=(jax.ShapeDtypeStruct((B, S, HG_W), F32), jax.ShapeDtypeStruct((B, S, HG_W), F32)),
        scratch_shapes=[pltpu.VMEM((HG_W, HG_W), F32), pltpu.VMEM((HG_W, HG_W), F32)],
        compiler_params=_cparams(("parallel", "arbitrary")),
        name="hgrn",
    )(r3(hq), r3(lff), r3(hkf), r3(hv), r3(hq), r3(lfb), r3(hkb), r3(hv), *consts)
    return of.reshape(B * S, HG_W), ob.reshape(B * S, HG_W)


LRU_HALF = LRU_W // 2
LRU_HALO = 16


def _scan_chunk(a, b, carry, reverse):
    T, C = a.shape
    nt = T // 8
    av = a.reshape(nt, 8, C)
    bv = b.reshape(nt, 8, C)
    rid = lax.broadcasted_iota(jnp.int32, (nt, 8, C), 1)
    for d in (1, 2, 4):
        if reverse:
            m, sh = rid < 8 - d, 8 - d
        else:
            m, sh = rid >= d, d
        ad = pltpu.roll(av, sh, 1)
        bd = pltpu.roll(bv, sh, 1)
        bv = bv + av * jnp.where(m, bd, 0.0)
        av = av * jnp.where(m, ad, 1.0)
    hs = [None] * nt
    for i in (range(nt - 1, -1, -1) if reverse else range(nt)):
        h = av[i] * carry + bv[i]
        hs[i] = h
        carry = h[0:1] if reverse else h[7:8]
    return jnp.concatenate(hs, axis=0), carry


def _lru_kernel(lx_ref, lg_ref, cw_ref, cb_ref, w_ref, bias_ref, cneg_ref, out_ref, u_ref, hf_ref, hb_ref, *, S):
    T = T_LRU
    n = S // T
    C = LRU_HALF
    cw = cw_ref[...]
    cb = cb_ref[...]

    def conv_body(i, _):
        t0 = pl.multiple_of(i * T, T)
        cur = lx_ref[pl.ds(t0, T), :].astype(F32)
        ps = pl.multiple_of(jnp.maximum(t0 - LRU_HALO, 0), LRU_HALO)
        prev = jnp.where(i > 0, lx_ref[pl.ds(ps, LRU_HALO), :].astype(F32), 0.0)
        ns = pl.multiple_of(jnp.minimum(t0 + T, S - LRU_HALO), LRU_HALO)
        nxt = jnp.where(i < n - 1, lx_ref[pl.ds(ns, LRU_HALO), :].astype(F32), 0.0)
        ext = jnp.concatenate([prev, cur, nxt], axis=0)
        h = LRU_HALO
        u_ref[pl.ds(t0, T), :] = (cb + ext[h - 2:h - 2 + T] * cw[0:1] + ext[h - 1:h - 1 + T] * cw[1:2]
                                  + cur * cw[2:3] + ext[h + 1:h + 1 + T] * cw[3:4])
        return 0

    lax.fori_loop(0, n, conv_body, 0)

    def conv_u(i):
        return u_ref[pl.ds(pl.multiple_of(i * T, T), T), :]

    def gates(d, u):
        pre = jnp.dot(u.astype(BF16), w_ref[:, 2 * C * d:2 * C * (d + 1)], preferred_element_type=F32)
        r = _sigmoid(pre[:, :C] + bias_ref[2 * d:2 * d + 1, :])
        ig = _sigmoid(pre[:, C:] + bias_ref[2 * d + 1:2 * d + 2, :])
        log_a = cneg_ref[d:d + 1, :] * r
        a = jnp.exp(log_a)
        y = 1.0 - a * a
        b = jnp.where(y > 0.0, y * lax.rsqrt(y), 0.0) * (ig * u)
        return a, b

    def emit(rows, h):
        g = lg_ref[rows, :].astype(F32)
        out_ref[rows, :] = (h * (g * _sigmoid(g))).astype(BF16)

    def step(i, carry, meet):
        cf, cr = carry
        j = n - 1 - i
        rows_i = pl.ds(pl.multiple_of(i * T, T), T)
        rows_j = pl.ds(pl.multiple_of(j * T, T), T)
        af, bf = gates(0, conv_u(i))
        ar, br = gates(1, conv_u(j))
        hf, cf = _scan_chunk(af, bf, cf, False)
        hr, cr = _scan_chunk(ar, br, cr, True)
        if meet:
            emit(rows_i, hf + hb_ref[rows_i, :])
            emit(rows_j, hf_ref[rows_j, :] + hr)
        else:
            hf_ref[rows_i, :] = hf
            hb_ref[rows_j, :] = hr
        return cf, cr

    zero = jnp.zeros((1, C), F32)
    carry = lax.fori_loop(0, n // 2, functools.partial(step, meet=False), (zero, zero))
    lax.fori_loop(n // 2, n, functools.partial(step, meet=True), carry)


def _lru(lx, lg, B, S, cw, cb, w, bias, cneg):
    C = LRU_HALF
    seq = pl.BlockSpec((None, S, C), lambda b, c: (b, 0, c))
    vec = lambda r: pl.BlockSpec((r, C), lambda b, c: (0, c))
    out = pl.pallas_call(
        functools.partial(_lru_kernel, S=S),
        grid=(B, LRU_W // C),
        in_specs=[seq, seq, vec(4), vec(1), pl.BlockSpec((None, C, 4 * C), lambda b, c: (c, 0, 0)), vec(4), vec(2)],
        out_specs=seq,
        out_shape=jax.ShapeDtypeStruct((B, S, LRU_W), BF16),
        scratch_shapes=[pltpu.VMEM((S, C), F32), pltpu.VMEM((S, C), F32), pltpu.VMEM((S, C), F32)],
        compiler_params=_cparams(("parallel", "parallel")),
        name="lru",
    )(lx.reshape(B, S, LRU_W), lg.reshape(B, S, LRU_W), cw, cb, w, bias, cneg)
    return out.reshape(B * S, LRU_W)


PV_ROWS = DA_V + 16
M_INIT = -1e30
ATTN_GROUP = 2


def _attn_kernel(lam_ref, qT_ref, qTn_ref, k_ref, vT_ref, dg_ref, ng_ref, out_ref, rhs_ref, m_ref, acc_ref, sa_ref, sb_ref,
                 *, S, tk, lam_init):
    nk = S // tk
    qi = pl.program_id(1)
    cur_q = qi % 2
    nxt_q = 1 - cur_q
    head_of_row = lax.broadcasted_iota(jnp.int32, (DA_W, TQ), 0) // DA_QK
    ones = jnp.ones((PV_ROWS - DA_V, tk), BF16)

    def make_rhs(q_ref, slot):
        qT = q_ref[...].astype(F32)
        for hc in range(2 * HEADS):
            rhs_ref[slot, hc] = jnp.where(head_of_row == hc, qT, 0.0).astype(BF16)

    def stage(j_next, q_slot, next_ref, j_cur, cur_ref):
        k = k_ref[pl.ds(pl.multiple_of(j_next * tk, tk), tk), :]
        if j_cur is not None:
            vT = vT_ref[:, pl.ds(pl.multiple_of(j_cur * tk, tk), tk)]
        for hc in range(2 * HEADS):
            if hc % ATTN_GROUP == 0:
                for g in range(hc, hc + ATTN_GROUP):
                    next_ref[g] = jnp.dot(k, rhs_ref[q_slot, g], preferred_element_type=F32)
            if j_cur is not None:
                h = hc // 2
                s = cur_ref[hc]
                m_old = m_ref[hc]
                m_new = jnp.maximum(m_old, jnp.max(s, axis=0, keepdims=True))
                p = jnp.exp2(s - m_new).astype(BF16)
                lhs = jnp.concatenate([vT[DA_V * h:DA_V * (h + 1), :], ones], axis=0)
                acc_ref[hc] = jnp.exp2(m_old - m_new) * acc_ref[hc] + jnp.dot(lhs, p, preferred_element_type=F32)
                m_ref[hc] = m_new

    @pl.when(qi == 0)
    def _():
        make_rhs(qT_ref, 0)
        stage(0, 0, sa_ref, None, None)

    make_rhs(qTn_ref, nxt_q)
    m_ref[...] = jnp.full(m_ref.shape, M_INIT, F32)
    acc_ref[...] = jnp.zeros_like(acc_ref)

    def body(i, _):
        stage(2 * i + 1, cur_q, sb_ref, 2 * i, sa_ref)
        stage(2 * i + 2, cur_q, sa_ref, 2 * i + 1, sb_ref)
        return 0

    lax.fori_loop(0, nk // 2 - 1, body, 0)
    stage(nk - 1, cur_q, sb_ref, nk - 2, sa_ref)
    stage(0, nxt_q, sa_ref, nk - 1, sb_ref)

    lam = lam_ref[0]
    outs = []
    for h in range(HEADS):
        a0 = acc_ref[2 * h]
        a1 = acc_ref[2 * h + 1]
        o = a0[:DA_V] * (1.0 / a0[DA_V:DA_V + 1]) - lam * (a1[:DA_V] * (1.0 / a1[DA_V:DA_V + 1]))
        ms = jnp.mean(o * o, axis=0, keepdims=True)
        outs.append(o * lax.rsqrt(ms + RMS_EPS))
    o = jnp.concatenate(outs, axis=0).T
    g = dg_ref[...].astype(F32)
    out_ref[...] = (o * ng_ref[...] * (1.0 - lam_init) * (g * _sigmoid(g))).astype(BF16)


def _attn(lam, dqT, dk, dvT, dg, ng, B, S, lam_init):
    nq = S // TQ
    tk = min(TK, S // 2)
    out = pl.pallas_call(
        functools.partial(_attn_kernel, S=S, tk=tk, lam_init=lam_init),
        grid=(B, nq),
        in_specs=[
            pl.BlockSpec(memory_space=pltpu.SMEM),
            pl.BlockSpec((None, DA_W, TQ), lambda b, i: (b, 0, i)),
            pl.BlockSpec((None, DA_W, TQ), lambda b, i: (b, 0, jnp.minimum(i + 1, nq - 1))),
            pl.BlockSpec((None, S, DA_W), lambda b, i: (b, 0, 0)),
            pl.BlockSpec((None, DA_W, S), lambda b, i: (b, 0, 0)),
            pl.BlockSpec((None, TQ, DA_W), lambda b, i: (b, i, 0)),
            pl.BlockSpec((1, DA_W), lambda b, i: (0, 0)),
        ],
        out_specs=pl.BlockSpec((None, TQ, DA_W), lambda b, i: (b, i, 0)),
        out_shape=jax.ShapeDtypeStruct((B, S, DA_W), BF16),
        scratch_shapes=[
            pltpu.VMEM((2, 2 * HEADS, DA_W, TQ), BF16),
            pltpu.VMEM((2 * HEADS, 1, TQ), F32),
            pltpu.VMEM((2 * HEADS, PV_ROWS, TQ), F32),
            pltpu.VMEM((2 * HEADS, tk, TQ), F32),
            pltpu.VMEM((2 * HEADS, tk, TQ), F32),
        ],
        compiler_params=_cparams(("parallel", "arbitrary")),
        name="attn",
    )(lam, dqT, dqT, dk.reshape(B, S, DA_W), dvT, dg.reshape(B, S, DA_W), ng)
    return out.reshape(B * S, DA_W)


def _outproj_kernel(h_ref, of_ref, ob_ref, hg_ref, olru_ref, oda_ref, p_ref, hgn_ref, ones_ref,
                    wout_ref, gw_ref, pw_ref, fn_ref, out_ref, *, final):
    o = of_ref[...] + ob_ref[...]
    sq = o * o
    hi = sq.astype(BF16)
    lo = (sq - hi.astype(F32)).astype(BF16)
    ones = ones_ref[...]
    ssum = jnp.dot(hi, ones, preferred_element_type=F32) + jnp.dot(lo, ones, preferred_element_type=F32)
    g = hg_ref[...].astype(F32)
    ohg = (o * lax.rsqrt(ssum * (1.0 / 64.0) + RMS_EPS) * hgn_ref[...] * (g * _sigmoid(g))).astype(BF16)

    acc = h_ref[...]
    acc = acc + jnp.dot(ohg, wout_ref[0:HG_W, :], preferred_element_type=F32)
    acc = acc + jnp.dot(olru_ref[...], wout_ref[HG_W:HG_W + LRU_W, :], preferred_element_type=F32)
    acc = acc + jnp.dot(oda_ref[...], wout_ref[HG_W + LRU_W:, :], preferred_element_type=F32)
    gate = _sigmoid(jnp.dot(acc.astype(BF16), gw_ref[...], preferred_element_type=F32))
    h2 = acc + gate * jnp.dot(p_ref[...].astype(BF16), pw_ref[...], preferred_element_type=F32)
    if final:
        ms = jnp.mean(h2 * h2, axis=-1, keepdims=True)
        h2 = h2 * lax.rsqrt(ms + RMS_EPS) * fn_ref[...]
    out_ref[...] = h2


def _outproj(h, of, ob, hg, olru, oda, p, hgn, ones, wout, gw, pw, fn, final):
    R = h.shape[0]
    tm = min(TM_OUT, R)
    row = lambda w: pl.BlockSpec((tm, w), lambda i: (i, 0))
    full = lambda a: pl.BlockSpec(a.shape, lambda i: (0,) * a.ndim, pipeline_mode=pl.Buffered(1))
    return pl.pallas_call(
        functools.partial(_outproj_kernel, final=final),
        grid=(R // tm,),
        in_specs=[row(D_MODEL), row(HG_W), row(HG_W), row(HG_W), row(LRU_W), row(DA_W), row(PLE_DIM),
                  full(hgn), full(ones), full(wout), full(gw), full(pw), full(fn)],
        out_specs=row(D_MODEL),
        out_shape=jax.ShapeDtypeStruct((R, D_MODEL), F32),
        compiler_params=_cparams(("parallel",)),
        name="outproj",
    )(h, of, ob, hg, olru, oda, p, hgn, ones, wout, gw, pw, fn)


def _rope_tables(S):
    half = ROPE_DIM // 2
    inv = ROPE_THETA ** (-jnp.arange(half, dtype=F32) * 2.0 / ROPE_DIM)
    ang = jnp.arange(S, dtype=F32)[:, None] * inv[None, :]
    cos, sin = jnp.cos(ang), jnp.sin(ang)
    ones = jnp.ones((S, DA_QK - ROPE_DIM), F32)
    zeros = jnp.zeros((S, DA_QK - ROPE_DIM), F32)
    z4 = jnp.zeros((S, half), F32)
    rep = lambda t: jnp.tile(t, (1, LANES // DA_QK))
    return (rep(jnp.concatenate([cos, cos, ones], axis=1)),
            rep(jnp.concatenate([-sin, z4, zeros], axis=1)),
            rep(jnp.concatenate([z4, sin, zeros], axis=1)))


def _lru_gate_weights(wa, wx):
    per_half = LRU_HALF // LRU_BLOCK
    halves = []
    for c in range(LRU_W // LRU_HALF):
        tiles = []
        for d in range(2):
            for w in (wa, wx):
                blocks = [w[d, per_half * c + m] for m in range(per_half)]
                tiles.append(jax.scipy.linalg.block_diag(*blocks))
        halves.append(jnp.concatenate(tiles, axis=1))
    return jnp.stack(halves).astype(BF16)


def _trunk(x, p, B, S, prm):
    R = B * S
    h = x.reshape(R, D_MODEL)
    depth = prm["w_in"].shape[0]
    cos_t, s1_t, s2_t = _rope_tables(S)
    lb_all = jnp.cumsum(jax.nn.softmax(prm["hg_lb"].astype(F32), axis=0), axis=0)
    lb_all = lb_all - lb_all[:1]
    lane = np.arange(HG_W)
    ones_bd = jnp.asarray((lane[:, None] // 64) == (lane[None, :] // 64), BF16)
    for l in range(depth):
        w_in = prm["w_in"][l].astype(BF16)
        (hq, hkf, hkb, hv, hg, lff, lfb, lx, lg, dqT, dk, dvT, dg) = _inproj(
            h, B, S, prm["norm_g"][l][None, :], w_in[:, :D_HG_IN], w_in[:, D_HG_IN:D_HG_IN + D_LRU_IN],
            w_in[:, D_HG_IN + D_LRU_IN:], lb_all[l], cos_t, s1_t, s2_t)

        of, ob = _hgrn(hq, lff, hkf, hv, lfb, hkb, B, S)

        cneg = -LRU_C * jax.nn.softplus(-prm["lru_lam"][l].astype(F32))
        bias = jnp.stack([prm["lru_ba"][l, 0], prm["lru_bx"][l, 0], prm["lru_ba"][l, 1], prm["lru_bx"][l, 1]])
        olru = _lru(lx, lg, B, S, prm["lru_conv_w"][l], prm["lru_conv_b"][l][None, :],
                    _lru_gate_weights(prm["lru_wa"][l], prm["lru_wx"][l]), bias, cneg)

        lam_init = 0.8 - 0.6 * math.exp(-0.3 * l)
        lam = (jnp.exp(jnp.sum(prm["da_lq1"][l].astype(F32) * prm["da_lk1"][l].astype(F32)))
               - jnp.exp(jnp.sum(prm["da_lq2"][l].astype(F32) * prm["da_lk2"][l].astype(F32))) + lam_init)
        oda = _attn(lam.reshape(1), dqT, dk, dvT, dg, jnp.tile(prm["da_norm"][l], HEADS)[None, :], B, S, lam_init)

        h = _outproj(h, of, ob, hg, olru, oda, p[l].reshape(R, PLE_DIM), jnp.tile(prm["hg_norm"][l], HEADS)[None, :],
                     ones_bd, prm["w_out"][l].astype(BF16), prm["ple_gate_w"][l].astype(BF16),
                     prm["ple_w"][l].astype(BF16), prm["final_norm"][None, :], final=(l == depth - 1))
    return h.reshape(B, S, D_MODEL)


def kernel(x_prompt, x_sample, p_prompt, p_sample, norm_g, w_in, w_out, hg_lb, hg_norm, lru_conv_w, lru_conv_b,
           lru_wa, lru_ba, lru_wx, lru_bx, lru_lam, da_lq1, da_lk1, da_lq2, da_lk2, da_norm, ple_w, ple_gate_w,
           final_norm):
    prm = dict(norm_g=norm_g, w_in=w_in, w_out=w_out, hg_lb=hg_lb, hg_norm=hg_norm, lru_conv_w=lru_conv_w,
               lru_conv_b=lru_conv_b, lru_wa=lru_wa, lru_ba=lru_ba, lru_wx=lru_wx, lru_bx=lru_bx, lru_lam=lru_lam,
               da_lq1=da_lq1, da_lk1=da_lk1, da_lq2=da_lq2, da_lk2=da_lk2, da_norm=da_norm, ple_w=ple_w,
               ple_gate_w=ple_gate_w, final_norm=final_norm)
    outs = []
    for x, p in ((x_prompt, p_prompt), (x_sample, p_sample)):
        B, S, _ = x.shape
        outs.append(_trunk(x, p, B, S, prm))
    return tuple(outs)
```

```python
import functools
import math

import numpy as np
import jax
import jax.numpy as jnp
from jax import lax
from jax.experimental import pallas as pl
from jax.experimental.pallas import tpu as pltpu

F32 = jnp.float32
BF16 = jnp.bfloat16

D_MODEL = 1024
PLE_DIM = 256
HEADS = 4
HG_W = 256
HG_DK = HG_W // HEADS
LRU_W = 512
LRU_BLOCK = 64
LRU_C = 8.0
DA_W = 256
DA_QK = 32
DA_V = 64
ROPE_THETA = 500000.0
ROPE_DIM = 8
RMS_EPS = 1e-6
LOG2E = 1.4426950408889634
D_HG_IN = 5 * HG_W
D_LRU_IN = 2 * LRU_W
D_DA_IN = 4 * DA_W

LANES = 128
VMEM_LIMIT = 56 * 1024 * 1024

TM_IN = 1024
TM_OUT = 1024
T_HG = 256
HG_CHUNK = 64
HG_BLK = 16
T_LRU = 256
TQ = 256
TK = 512


def _sigmoid(x):
    return 1.0 / (1.0 + jnp.exp2(x * -LOG2E))


def _cparams(sem):
    return pltpu.CompilerParams(dimension_semantics=sem, vmem_limit_bytes=VMEM_LIMIT)


def _rope(t, cos, s1, s2):
    halves = []
    for j in range(2):
        th = t[:, LANES * j:LANES * (j + 1)]
        up = pltpu.roll(th, LANES - ROPE_DIM // 2, 1)
        dn = pltpu.roll(th, ROPE_DIM // 2, 1)
        halves.append(th * cos + up * s1 + dn * s2)
    return jnp.concatenate(halves, axis=1)


def _inproj_kernel(x_ref, g_ref, whg_ref, wlru_ref, wda_ref, lb_ref, cos_ref, s1_ref, s2_ref,
                   hq_ref, hkf_ref, hkb_ref, hv_ref, hg_ref, lff_ref, lfb_ref, lx_ref, lg_ref,
                   dqT_ref, dk_ref, dvT_ref, dg_ref):
    x = x_ref[...]
    ms = jnp.mean(x * x, axis=-1, keepdims=True)
    hn = (x * lax.rsqrt(ms + RMS_EPS) * g_ref[...]).astype(BF16)

    z = jnp.dot(hn, whg_ref[...], preferred_element_type=F32)
    zq = z[:, 0:HG_W]
    hq_ref[...] = (zq * _sigmoid(zq)).astype(BF16)
    for d, lf_ref, k_ref in ((0, lff_ref, hkf_ref), (1, lfb_ref, hkb_ref)):
        zf = z[:, HG_W * (1 + d):HG_W * (2 + d)]
        lb = lb_ref[d:d + 1, :]
        s = _sigmoid(zf)
        lf_ref[...] = jnp.log2(lb + (1.0 - lb) * s)
        k_ref[...] = ((1.0 - lb) * (1.0 - s)).astype(BF16)
    hv_ref[...] = z[:, 3 * HG_W:4 * HG_W].astype(BF16)
    hg_ref[...] = z[:, 4 * HG_W:5 * HG_W].astype(BF16)

    z = jnp.dot(hn, wlru_ref[...], preferred_element_type=F32)
    lx_ref[...] = z[:, :LRU_W].astype(BF16)
    lg_ref[...] = z[:, LRU_W:].astype(BF16)

    z = jnp.dot(hn, wda_ref[...], preferred_element_type=F32)
    cos, s1, s2 = cos_ref[...], s1_ref[...], s2_ref[...]
    q = _rope(z[:, 0:DA_W], cos, s1, s2) * (DA_QK ** -0.5 * LOG2E)
    k = _rope(z[:, DA_W:2 * DA_W], cos, s1, s2)
    dqT_ref[...] = q.T.astype(BF16)
    dk_ref[...] = k.astype(BF16)
    dvT_ref[...] = z[:, 2 * DA_W:3 * DA_W].T.astype(BF16)
    dg_ref[...] = z[:, 3 * DA_W:4 * DA_W].astype(BF16)


def _inproj(x2d, B, S, g, whg, wlru, wda, lb, cos_t, s1_t, s2_t):
    R = B * S
    tm = min(TM_IN, S)
    nt = S // tm
    row = lambda w: pl.BlockSpec((tm, w), lambda i: (i, 0))
    full = lambda a: pl.BlockSpec(a.shape, lambda i: (0,) * a.ndim, pipeline_mode=pl.Buffered(1))
    tab = pl.BlockSpec((tm, LANES), lambda i: (i % nt, 0))
    tr = pl.BlockSpec((None, DA_W, tm), lambda i: (i // nt, 0, i % nt))
    sds = jax.ShapeDtypeStruct
    out_shape = (
        sds((R, HG_W), BF16), sds((R, HG_W), BF16), sds((R, HG_W), BF16), sds((R, HG_W), BF16), sds((R, HG_W), BF16),
        sds((R, HG_W), F32), sds((R, HG_W), F32),
        sds((R, LRU_W), BF16), sds((R, LRU_W), BF16),
        sds((B, DA_W, S), BF16), sds((R, DA_W), BF16), sds((B, DA_W, S), BF16), sds((R, DA_W), BF16),
    )
    out_specs = (
        row(HG_W), row(HG_W), row(HG_W), row(HG_W), row(HG_W), row(HG_W), row(HG_W),
        row(LRU_W), row(LRU_W),
        tr, row(DA_W), tr, row(DA_W),
    )
    return pl.pallas_call(
        _inproj_kernel,
        grid=(R // tm,),
        in_specs=[row(D_MODEL), full(g), full(whg), full(wlru), full(wda), full(lb), tab, tab, tab],
        out_specs=out_specs,
        out_shape=out_shape,
        compiler_params=_cparams(("parallel",)),
        name="inproj",
    )(x2d, g, whg, wlru, wda, lb, cos_t, s1_t, s2_t)


def _split2(x):
    hi = x.astype(BF16)
    lo = (x - hi.astype(F32)).astype(BF16)
    return hi, lo


def _dot_t(a, b):
    return lax.dot_general(a, b, (((1,), (1,)), ((), ())), preferred_element_type=F32)


def _dot_a(a, b):
    return lax.dot_general(a, b, (((0,), (0,)), ((), ())), preferred_element_type=F32)


def _head_rows(x_bf, hm_ref):
    zeros = jnp.zeros((HG_BLK, LANES), BF16)
    rows = []
    for h in range(HEADS):
        half = x_bf[:, LANES * (h // 2):LANES * (h // 2 + 1)] * hm_ref[h % 2]
        rows.append(jnp.concatenate([half, zeros] if h < 2 else [zeros, half], axis=1))
    return jnp.concatenate(rows, axis=0)


def _hgrn_tile(q_ref, lf_ref, k_ref, v_ref, ltri_ref, esel_ref, gmask_ref, hm_ref, bd_ref, s_ref, o_ref, reverse):
    T = T_HG
    nblk = HG_CHUNK // HG_BLK
    q = q_ref[...].astype(F32)
    k = k_ref[...].astype(F32)
    v_bf = v_ref[...]
    lf = lf_ref[...]

    ltri = ltri_ref[...]
    c = sum(jnp.dot(ltri, piece, preferred_element_type=F32) for piece in _split2(lf))

    nb = T // HG_BLK
    half = HG_BLK // 2
    c4 = c.reshape(nb, 2, half, HG_W)
    ck4 = (c - jnp.log2(k)).reshape(nb, 2, half, HG_W)
    q4 = q.reshape(nb, 2, half, HG_W)
    far = 0 if reverse else 1
    dacc = jnp.zeros((T, HG_W), F32)
    dfar = jnp.zeros((T // 2, HG_W), F32)
    for jl in range(HG_BLK):
        cj = ck4[:, jl // half, jl % half:jl % half + 1, :][:, None]
        if jl // half == far:
            xj = q4[:, far:far + 1] * jnp.exp2(jnp.minimum(c4[:, far:far + 1] - cj, 0.0))
            dfar = dfar + jnp.dot(xj.reshape(T // 2, HG_W).astype(BF16), esel_ref[jl], preferred_element_type=F32)
        else:
            near = 1 - far
            x_near = q4[:, near:near + 1] * jnp.exp2(jnp.minimum(c4[:, near:near + 1] - cj, 0.0))
            x_far = q4[:, far:far + 1] * jnp.exp2(c4[:, far:far + 1] - cj)
            xj = jnp.concatenate([x_near, x_far] if far else [x_far, x_near], axis=1)
            dacc = dacc + jnp.dot(xj.reshape(T, HG_W).astype(BF16), esel_ref[jl], preferred_element_type=F32)
    d4 = dacc.reshape(nb, 2, half, HG_W)
    dfar4 = d4[:, far:far + 1] + dfar.reshape(nb, 1, half, HG_W)
    d4 = jnp.concatenate([d4[:, 0:1], dfar4] if far else [dfar4, d4[:, 1:2]], axis=1)
    dacc = d4.reshape(T, HG_W) * gmask_ref[...]
    yield

    bd = bd_ref[...]
    zpad64 = jnp.zeros((HG_CHUNK, HG_W), BF16)
    zcol = jnp.zeros((HG_BLK, LANES), F32)
    chunks = list(range(T // HG_CHUNK))
    per_chunk = {}
    for n in chunks:
        r0 = n * HG_CHUNK
        cn = c[r0:r0 + HG_CHUNK]
        qn = q[r0:r0 + HG_CHUNK]
        kn = k[r0:r0 + HG_CHUNK]
        vn = v_bf[r0:r0 + HG_CHUNK]
        ctot = cn[0:1] if reverse else cn[HG_CHUNK - 1:HG_CHUNK]
        qt = (qn * jnp.exp2(cn)).astype(BF16)
        kt = (kn * jnp.exp2(ctot - cn)).astype(BF16)
        upd = bd * _dot_a(vn, kt)
        vbd = jnp.concatenate([_head_rows(vn[HG_BLK * b:HG_BLK * (b + 1)], hm_ref) for b in range(nblk)], axis=0)
        a_rows = []
        for i in range(nblk):
            rows = slice(HG_BLK * i, HG_BLK * (i + 1))
            d_i = dacc[r0 + HG_BLK * i:r0 + HG_BLK * (i + 1)]
            others = list(range(i + 1, nblk)) if reverse else list(range(i))
            if not others:
                a_rows.append(d_i)
                continue
            ref_row = cn[HG_BLK * (i + 1):HG_BLK * (i + 1) + 1] if reverse else cn[HG_BLK * i - 1:HG_BLK * i]
            qh = (qn[rows] * jnp.exp2(cn[rows] - ref_row)).astype(BF16)
            pieces = []
            for b in others:
                brow = slice(HG_BLK * b, HG_BLK * (b + 1))
                kh = (kn[brow] * jnp.exp2(ref_row - cn[brow])).astype(BF16)
                pieces.append(_head_rows(kh, hm_ref))
            if len(pieces) % 2:
                pieces = ([zpad64] + pieces) if reverse else (pieces + [zpad64])
            r_i = _dot_t(qh, jnp.concatenate(pieces, axis=0))
            if r_i.shape[1] == LANES:
                r_i = jnp.concatenate([zcol, r_i] if reverse else [r_i, zcol], axis=1)
            a_rows.append(r_i + d_i)
        a_t = jnp.concatenate(a_rows, axis=0).astype(BF16)
        per_chunk[n] = (qt, jnp.exp2(ctot), upd, a_t, vbd)
    yield

    o_intra = {n: jnp.dot(per_chunk[n][3], per_chunk[n][4], preferred_element_type=F32) for n in chunks}
    s_t = s_ref[...]
    for n in (reversed(chunks) if reverse else chunks):
        qt, decay, upd, _, _ = per_chunk[n]
        r0 = n * HG_CHUNK
        o_ref[r0:r0 + HG_CHUNK, :] = _dot_t(qt, s_t.astype(BF16)) + o_intra[n]
        s_t = s_t * decay + upd
    s_ref[...] = s_t
    yield


def _hgrn_kernel(qf_ref, lff_ref, kf_ref, vf_ref, qb_ref, lfb_ref, kb_ref, vb_ref,
                 ltf_ref, ltb_ref, esel_ref, gmf_ref, gmb_ref, hm_ref, bd_ref,
                 of_ref, ob_ref, sf_ref, sb_ref):
    @pl.when(pl.program_id(1) == 0)
    def _():
        sf_ref[...] = jnp.zeros_like(sf_ref)
        sb_ref[...] = jnp.zeros_like(sb_ref)

    tiles = [
        _hgrn_tile(qf_ref, lff_ref, kf_ref, vf_ref, ltf_ref, esel_ref, gmf_ref, hm_ref, bd_ref, sf_ref, of_ref, False),
        _hgrn_tile(qb_ref, lfb_ref, kb_ref, vb_ref, ltb_ref, esel_ref, gmb_ref, hm_ref, bd_ref, sb_ref, ob_ref, True),
    ]
    for _ in range(3):
        for t in tiles:
            next(t)


@functools.lru_cache(maxsize=None)
def _hgrn_consts():
    T = T_HG
    t = np.arange(T)
    same_chunk = (t[:, None] // HG_CHUNK) == (t[None, :] // HG_CHUNK)
    ltf = (same_chunk & (t[None, :] <= t[:, None])).astype(np.float32)
    ltb = (same_chunk & (t[None, :] >= t[:, None])).astype(np.float32)
    lane = np.arange(HG_W)
    col = np.arange(HG_W)
    esel = np.zeros((HG_BLK, HG_W, HG_W), np.float32)
    grp = HEADS * HG_BLK
    for jl in range(HG_BLK):
        esel[jl] = ((lane[:, None] // HG_DK) == ((col[None, :] % grp) // HG_BLK)) & ((col[None, :] % HG_BLK) == jl)
    blk_of_row = (t % HG_CHUNK) // HG_BLK
    own = (col[None, :] // grp) == blk_of_row[:, None]
    jl_col = col[None, :] % HG_BLK
    gmf = (own & (jl_col <= (t % HG_BLK)[:, None])).astype(np.float32)
    gmb = (own & (jl_col >= (t % HG_BLK)[:, None])).astype(np.float32)
    hm = np.stack([(np.arange(LANES) < HG_DK), (np.arange(LANES) >= HG_DK)]).astype(np.float32)
    hm = np.broadcast_to(hm[:, None, :], (2, HG_BLK, LANES)).copy()
    bd = ((lane[:, None] // HG_DK) == (lane[None, :] // HG_DK)).astype(np.float32)
    return ltf, ltb, esel, gmf, gmb, hm, bd


def _hgrn(hq, lff, hkf, hv, lfb, hkb, B, S):
    T = T_HG
    nt = S // T
    ltf, ltb, esel, gmf, gmb, hm, bd = _hgrn_consts()
    consts = (jnp.asarray(ltf, BF16), jnp.asarray(ltb, BF16), jnp.asarray(esel, BF16),
              jnp.asarray(gmf, F32), jnp.asarray(gmb, F32), jnp.asarray(hm, BF16), jnp.asarray(bd, F32))
    r3 = lambda a: a.reshape(B, S, HG_W)
    fwd = pl.BlockSpec((None, T, HG_W), lambda b, i: (b, i, 0))
    bwd = pl.BlockSpec((None, T, HG_W), lambda b, i: (b, nt - 1 - i, 0))
    full = lambda a: pl.BlockSpec(a.shape, lambda b, i: (0,) * a.ndim, pipeline_mode=pl.Buffered(1))
    of, ob = pl.pallas_call(
        _hgrn_kernel,
        grid=(B, nt),
        in_specs=[fwd, fwd, fwd, fwd, bwd, bwd, bwd, bwd] + [full(a) for a in consts],
        out_specs=(fwd, bwd),
        out_shape=(jax.ShapeDtypeStruct((B, S, HG_W), F32), jax.ShapeDtypeStruct((B, S, HG_W), F32)),
        scratch_shapes=[pltpu.VMEM((HG_W, HG_W), F32), pltpu.VMEM((HG_W, HG_W), F32)],
        compiler_params=_cparams(("parallel", "arbitrary")),
        name="hgrn",
    )(r3(hq), r3(lff), r3(hkf), r3(hv), r3(hq), r3(lfb), r3(hkb), r3(hv), *consts)
    return of.reshape(B * S, HG_W), ob.reshape(B * S, HG_W)


LRU_HALF = LRU_W // 2
LRU_HALO = 16


SUBLANES = 8
LRU_SEG = 16
LRU_NSEG = T_LRU // LRU_SEG
LRU_PITCH = LRU_SEG + SUBLANES


def _scan_chunk(a, b, carry, reverse, a_s, b_s, h_s):
    T, C = a.shape
    L, P, G = LRU_SEG, LRU_PITCH, LRU_NSEG // SUBLANES
    outs, carries = [], []
    for slab in range(C // LANES):
        lanes = slice(LANES * slab, LANES * (slab + 1))
        for s in range(LRU_NSEG):
            a_s[slab, P * s:P * s + L, :] = a[L * s:L * (s + 1), lanes]
            b_s[slab, P * s:P * s + L, :] = b[L * s:L * (s + 1), lanes]
        h = [jnp.zeros((SUBLANES, LANES), F32)] * G
        p = [jnp.ones((SUBLANES, LANES), F32)] * G
        h_loc = [[None] * L for _ in range(G)]
        p_loc = [[None] * L for _ in range(G)]
        for v in (range(L - 1, -1, -1) if reverse else range(L)):
            for g in range(G):
                rows = pl.ds(P * SUBLANES * g + v, SUBLANES, stride=P)
                av = a_s[slab, rows, :]
                h[g] = av * h[g] + b_s[slab, rows, :]
                p[g] = av * p[g]
                h_loc[g][v], p_loc[g][v] = h[g], p[g]
        c = carry[:, lanes]
        inflow = [None] * LRU_NSEG
        for s in (range(LRU_NSEG - 1, -1, -1) if reverse else range(LRU_NSEG)):
            g, r = divmod(s, SUBLANES)
            inflow[s] = c
            c = p[g][r:r + 1] * c + h[g][r:r + 1]
        for g in range(G):
            cin = jnp.concatenate(inflow[SUBLANES * g:SUBLANES * (g + 1)], axis=0)
            for v in range(L):
                h_s[slab, pl.ds(P * SUBLANES * g + v, SUBLANES, stride=P), :] = h_loc[g][v] + p_loc[g][v] * cin
        outs.append(jnp.concatenate([h_s[slab, P * s:P * s + L, :] for s in range(LRU_NSEG)], axis=0))
        carries.append(c)
    return jnp.concatenate(outs, axis=1), jnp.concatenate(carries, axis=1)


def _lru_kernel(lx_ref, lg_ref, cw_ref, cb_ref, w_ref, bias_ref, cneg_ref, out_ref, u_ref, hf_ref, hb_ref,
                *stage_refs, S):
    T = T_LRU
    n = S // T
    C = LRU_HALF
    cw = cw_ref[...]
    cb = cb_ref[...]

    def conv_body(i, _):
        t0 = pl.multiple_of(i * T, T)
        cur = lx_ref[pl.ds(t0, T), :].astype(F32)
        ps = pl.multiple_of(jnp.maximum(t0 - LRU_HALO, 0), LRU_HALO)
        prev = jnp.where(i > 0, lx_ref[pl.ds(ps, LRU_HALO), :].astype(F32), 0.0)
        ns = pl.multiple_of(jnp.minimum(t0 + T, S - LRU_HALO), LRU_HALO)
        nxt = jnp.where(i < n - 1, lx_ref[pl.ds(ns, LRU_HALO), :].astype(F32), 0.0)
        ext = jnp.concatenate([prev, cur, nxt], axis=0)
        h = LRU_HALO
        u_ref[pl.ds(t0, T), :] = (cb + ext[h - 2:h - 2 + T] * cw[0:1] + ext[h - 1:h - 1 + T] * cw[1:2]
                                  + cur * cw[2:3] + ext[h + 1:h + 1 + T] * cw[3:4])
        return 0

    lax.fori_loop(0, n, conv_body, 0)

    def conv_u(i):
        return u_ref[pl.ds(pl.multiple_of(i * T, T), T), :]

    def gates(d, u):
        pre = jnp.dot(u.astype(BF16), w_ref[:, 2 * C * d:2 * C * (d + 1)], preferred_element_type=F32)
        r = _sigmoid(pre[:, :C] + bias_ref[2 * d:2 * d + 1, :])
        ig = _sigmoid(pre[:, C:] + bias_ref[2 * d + 1:2 * d + 2, :])
        log_a = cneg_ref[d:d + 1, :] * r
        a = jnp.exp(log_a)
        y = 1.0 - a * a
        b = jnp.where(y > 0.0, y * lax.rsqrt(y), 0.0) * (ig * u)
        return a, b

    def emit(rows, h):
        g = lg_ref[rows, :].astype(F32)
        out_ref[rows, :] = (h * (g * _sigmoid(g))).astype(BF16)

    def step(i, carry, meet):
        cf, cr = carry
        j = n - 1 - i
        rows_i = pl.ds(pl.multiple_of(i * T, T), T)
        rows_j = pl.ds(pl.multiple_of(j * T, T), T)
        af, bf = gates(0, conv_u(i))
        ar, br = gates(1, conv_u(j))
        hf, cf = _scan_chunk(af, bf, cf, False, *stage_refs[0:3])
        hr, cr = _scan_chunk(ar, br, cr, True, *stage_refs[3:6])
        if meet:
            emit(rows_i, hf + hb_ref[rows_i, :])
            emit(rows_j, hf_ref[rows_j, :] + hr)
        else:
            hf_ref[rows_i, :] = hf
            hb_ref[rows_j, :] = hr
        return cf, cr

    zero = jnp.zeros((1, C), F32)
    unroll = 2 if n % 4 == 0 else 1
    carry = lax.fori_loop(0, n // 2, functools.partial(step, meet=False), (zero, zero), unroll=unroll)
    lax.fori_loop(n // 2, n, functools.partial(step, meet=True), carry, unroll=unroll)


def _lru(lx, lg, B, S, cw, cb, w, bias, cneg):
    C = LRU_HALF
    seq = pl.BlockSpec((None, S, C), lambda b, c: (b, 0, c))
    vec = lambda r: pl.BlockSpec((r, C), lambda b, c: (0, c))
    out = pl.pallas_call(
        functools.partial(_lru_kernel, S=S),
        grid=(B, LRU_W // C),
        in_specs=[seq, seq, vec(4), vec(1), pl.BlockSpec((None, C, 4 * C), lambda b, c: (c, 0, 0)), vec(4), vec(2)],
        out_specs=seq,
        out_shape=jax.ShapeDtypeStruct((B, S, LRU_W), BF16),
        scratch_shapes=[pltpu.VMEM((S, C), F32), pltpu.VMEM((S, C), F32), pltpu.VMEM((S, C), F32)]
        + [pltpu.VMEM((C // LANES, LRU_NSEG * LRU_PITCH, LANES), F32)] * 6,
        compiler_params=_cparams(("parallel", "parallel")),
        name="lru",
    )(lx.reshape(B, S, LRU_W), lg.reshape(B, S, LRU_W), cw, cb, w, bias, cneg)
    return out.reshape(B * S, LRU_W)


PV_ROWS = DA_V + 16
M_INIT = -1e30
ATTN_GROUP = 2


def _attn_kernel(lam_ref, qT_ref, qTn_ref, k_ref, vT_ref, dg_ref, ng_ref, out_ref, rhs_ref, m_ref, acc_ref, sa_ref, sb_ref,
                 *, S, tk, lam_init):
    nk = S // tk
    qi = pl.program_id(1)
    cur_q = qi % 2
    nxt_q = 1 - cur_q
    head_of_row = lax.broadcasted_iota(jnp.int32, (DA_W, TQ), 0) // DA_QK
    ones = jnp.ones((PV_ROWS - DA_V, tk), BF16)

    def make_rhs(q_ref, slot):
        qT = q_ref[...].astype(F32)
        for hc in range(2 * HEADS):
            rhs_ref[slot, hc] = jnp.where(head_of_row == hc, qT, 0.0).astype(BF16)

    def stage(j_next, q_slot, next_ref, j_cur, cur_ref):
        k = k_ref[pl.ds(pl.multiple_of(j_next * tk, tk), tk), :]
        if j_cur is not None:
            vT = vT_ref[:, pl.ds(pl.multiple_of(j_cur * tk, tk), tk)]
        for hc in range(2 * HEADS):
            if hc % ATTN_GROUP == 0:
                for g in range(hc, hc + ATTN_GROUP):
                    next_ref[g] = jnp.dot(k, rhs_ref[q_slot, g], preferred_element_type=F32)
            if j_cur is not None:
                h = hc // 2
                s = cur_ref[hc]
                m_old = m_ref[hc]
                m_new = jnp.maximum(m_old, jnp.max(s, axis=0, keepdims=True))
                p = jnp.exp2(s - m_new).astype(BF16)
                lhs = jnp.concatenate([vT[DA_V * h:DA_V * (h + 1), :], ones], axis=0)
                acc_ref[hc] = jnp.exp2(m_old - m_new) * acc_ref[hc] + jnp.dot(lhs, p, preferred_element_type=F32)
                m_ref[hc] = m_new

    @pl.when(qi == 0)
    def _():
        make_rhs(qT_ref, 0)
        stage(0, 0, sa_ref, None, None)

    make_rhs(qTn_ref, nxt_q)
    m_ref[...] = jnp.full(m_ref.shape, M_INIT, F32)
    acc_ref[...] = jnp.zeros_like(acc_ref)

    def body(i, _):
        stage(2 * i + 1, cur_q, sb_ref, 2 * i, sa_ref)
        stage(2 * i + 2, cur_q, sa_ref, 2 * i + 1, sb_ref)
        return 0

    lax.fori_loop(0, nk // 2 - 1, body, 0)
    stage(nk - 1, cur_q, sb_ref, nk - 2, sa_ref)
    stage(0, nxt_q, sa_ref, nk - 1, sb_ref)

    lam = lam_ref[0]
    outs = []
    for h in range(HEADS):
        a0 = acc_ref[2 * h]
        a1 = acc_ref[2 * h + 1]
        o = a0[:DA_V] * (1.0 / a0[DA_V:DA_V + 1]) - lam * (a1[:DA_V] * (1.0 / a1[DA_V:DA_V + 1]))
        ms = jnp.mean(o * o, axis=0, keepdims=True)
        outs.append(o * lax.rsqrt(ms + RMS_EPS))
    o = jnp.concatenate(outs, axis=0).T
    g = dg_ref[...].astype(F32)
    out_ref[...] = (o * ng_ref[...] * (1.0 - lam_init) * (g * _sigmoid(g))).astype(BF16)


def _attn(lam, dqT, dk, dvT, dg, ng, B, S, lam_init):
    nq = S // TQ
    tk = min(TK, S // 2)
    out = pl.pallas_call(
        functools.partial(_attn_kernel, S=S, tk=tk, lam_init=lam_init),
        grid=(B, nq),
        in_specs=[
            pl.BlockSpec(memory_space=pltpu.SMEM),
            pl.BlockSpec((None, DA_W, TQ), lambda b, i: (b, 0, i)),
            pl.BlockSpec((None, DA_W, TQ), lambda b, i: (b, 0, jnp.minimum(i + 1, nq - 1))),
            pl.BlockSpec((None, S, DA_W), lambda b, i: (b, 0, 0)),
            pl.BlockSpec((None, DA_W, S), lambda b, i: (b, 0, 0)),
            pl.BlockSpec((None, TQ, DA_W), lambda b, i: (b, i, 0)),
            pl.BlockSpec((1, DA_W), lambda b, i: (0, 0)),
        ],
        out_specs=pl.BlockSpec((None, TQ, DA_W), lambda b, i: (b, i, 0)),
        out_shape=jax.ShapeDtypeStruct((B, S, DA_W), BF16),
        scratch_shapes=[
            pltpu.VMEM((2, 2 * HEADS, DA_W, TQ), BF16),
            pltpu.VMEM((2 * HEADS, 1, TQ), F32),
            pltpu.VMEM((2 * HEADS, PV_ROWS, TQ), F32),
            pltpu.VMEM((2 * HEADS, tk, TQ), F32),
            pltpu.VMEM((2 * HEADS, tk, TQ), F32),
        ],
        compiler_params=_cparams(("parallel", "arbitrary")),
        name="attn",
    )(lam, dqT, dqT, dk.reshape(B, S, DA_W), dvT, dg.reshape(B, S, DA_W), ng)
    return out.reshape(B * S, DA_W)


def _outproj_kernel(h_ref, of_ref, ob_ref, hg_ref, olru_ref, oda_ref, p_ref, hgn_ref, ones_ref,
                    wout_ref, gw_ref, pw_ref, fn_ref, out_ref, *, final):
    o = of_ref[...] + ob_ref[...]
    sq = o * o
    hi = sq.astype(BF16)
    lo = (sq - hi.astype(F32)).astype(BF16)
    ones = ones_ref[...]
    ssum = jnp.dot(hi, ones, preferred_element_type=F32) + jnp.dot(lo, ones, preferred_element_type=F32)
    g = hg_ref[...].astype(F32)
    ohg = (o * lax.rsqrt(ssum * (1.0 / HG_DK) + RMS_EPS) * hgn_ref[...] * (g * _sigmoid(g))).astype(BF16)

    acc = h_ref[...]
    acc = acc + jnp.dot(ohg, wout_ref[0:HG_W, :], preferred_element_type=F32)
    acc = acc + jnp.dot(olru_ref[...], wout_ref[HG_W:HG_W + LRU_W, :], preferred_element_type=F32)
    acc = acc + jnp.dot(oda_ref[...], wout_ref[HG_W + LRU_W:, :], preferred_element_type=F32)
    gate = _sigmoid(jnp.dot(acc.astype(BF16), gw_ref[...], preferred_element_type=F32))
    h2 = acc + gate * jnp.dot(p_ref[...].astype(BF16), pw_ref[...], preferred_element_type=F32)
    if final:
        ms = jnp.mean(h2 * h2, axis=-1, keepdims=True)
        h2 = h2 * lax.rsqrt(ms + RMS_EPS) * fn_ref[...]
    out_ref[...] = h2


def _outproj(h, of, ob, hg, olru, oda, p, hgn, ones, wout, gw, pw, fn, final):
    R = h.shape[0]
    tm = min(TM_OUT, R)
    row = lambda w: pl.BlockSpec((tm, w), lambda i: (i, 0))
    full = lambda a: pl.BlockSpec(a.shape, lambda i: (0,) * a.ndim, pipeline_mode=pl.Buffered(1))
    return pl.pallas_call(
        functools.partial(_outproj_kernel, final=final),
        grid=(R // tm,),
        in_specs=[row(D_MODEL), row(HG_W), row(HG_W), row(HG_W), row(LRU_W), row(DA_W), row(PLE_DIM),
                  full(hgn), full(ones), full(wout), full(gw), full(pw), full(fn)],
        out_specs=row(D_MODEL),
        out_shape=jax.ShapeDtypeStruct((R, D_MODEL), F32),
        compiler_params=_cparams(("parallel",)),
        name="outproj",
    )(h, of, ob, hg, olru, oda, p, hgn, ones, wout, gw, pw, fn)


def _rope_tables(S):
    half = ROPE_DIM // 2
    inv = ROPE_THETA ** (-jnp.arange(half, dtype=F32) * 2.0 / ROPE_DIM)
    ang = jnp.arange(S, dtype=F32)[:, None] * inv[None, :]
    cos, sin = jnp.cos(ang), jnp.sin(ang)
    ones = jnp.ones((S, DA_QK - ROPE_DIM), F32)
    zeros = jnp.zeros((S, DA_QK - ROPE_DIM), F32)
    z4 = jnp.zeros((S, half), F32)
    rep = lambda t: jnp.tile(t, (1, LANES // DA_QK))
    return (rep(jnp.concatenate([cos, cos, ones], axis=1)),
            rep(jnp.concatenate([-sin, z4, zeros], axis=1)),
            rep(jnp.concatenate([z4, sin, zeros], axis=1)))


def _lru_gate_weights(wa, wx):
    per_half = LRU_HALF // LRU_BLOCK
    halves = []
    for c in range(LRU_W // LRU_HALF):
        tiles = []
        for d in range(2):
            for w in (wa, wx):
                blocks = [w[d, per_half * c + m] for m in range(per_half)]
                tiles.append(jax.scipy.linalg.block_diag(*blocks))
        halves.append(jnp.concatenate(tiles, axis=1))
    return jnp.stack(halves).astype(BF16)


def _trunk(x, p, B, S, prm):
    R = B * S
    h = x.reshape(R, D_MODEL)
    depth = prm["w_in"].shape[0]
    cos_t, s1_t, s2_t = _rope_tables(S)
    lb_all = jnp.cumsum(jax.nn.softmax(prm["hg_lb"].astype(F32), axis=0), axis=0)
    lb_all = lb_all - lb_all[:1]
    lane = np.arange(HG_W)
    ones_bd = jnp.asarray((lane[:, None] // HG_DK) == (lane[None, :] // HG_DK), BF16)
    for l in range(depth):
        w_in = prm["w_in"][l].astype(BF16)
        (hq, hkf, hkb, hv, hg, lff, lfb, lx, lg, dqT, dk, dvT, dg) = _inproj(
            h, B, S, prm["norm_g"][l][None, :], w_in[:, :D_HG_IN], w_in[:, D_HG_IN:D_HG_IN + D_LRU_IN],
            w_in[:, D_HG_IN + D_LRU_IN:], lb_all[l], cos_t, s1_t, s2_t)

        of, ob = _hgrn(hq, lff, hkf, hv, lfb, hkb, B, S)

        cneg = -LRU_C * jax.nn.softplus(-prm["lru_lam"][l].astype(F32))
        bias = jnp.stack([prm["lru_ba"][l, 0], prm["lru_bx"][l, 0], prm["lru_ba"][l, 1], prm["lru_bx"][l, 1]])
        olru = _lru(lx, lg, B, S, prm["lru_conv_w"][l], prm["lru_conv_b"][l][None, :],
                    _lru_gate_weights(prm["lru_wa"][l], prm["lru_wx"][l]), bias, cneg)

        lam_init = 0.8 - 0.6 * math.exp(-0.3 * l)
        lam = (jnp.exp(jnp.sum(prm["da_lq1"][l].astype(F32) * prm["da_lk1"][l].astype(F32)))
               - jnp.exp(jnp.sum(prm["da_lq2"][l].astype(F32) * prm["da_lk2"][l].astype(F32))) + lam_init)
        oda = _attn(lam.reshape(1), dqT, dk, dvT, dg, jnp.tile(prm["da_norm"][l], HEADS)[None, :], B, S, lam_init)

        h = _outproj(h, of, ob, hg, olru, oda, p[l].reshape(R, PLE_DIM), jnp.tile(prm["hg_norm"][l], HEADS)[None, :],
                     ones_bd, prm["w_out"][l].astype(BF16), prm["ple_gate_w"][l].astype(BF16),
                     prm["ple_w"][l].astype(BF16), prm["final_norm"][None, :], final=(l == depth - 1))
    return h.reshape(B, S, D_MODEL)


def kernel(x_prompt, x_sample, p_prompt, p_sample, norm_g, w_in, w_out, hg_lb, hg_norm, lru_conv_w, lru_conv_b,
           lru_wa, lru_ba, lru_wx, lru_bx, lru_lam, da_lq1, da_lk1, da_lq2, da_lk2, da_norm, ple_w, ple_gate_w,
           final_norm):
    prm = dict(norm_g=norm_g, w_in=w_in, w_out=w_out, hg_lb=hg_lb, hg_norm=hg_norm, lru_conv_w=lru_conv_w,
               lru_conv_b=lru_conv_b, lru_wa=lru_wa, lru_ba=lru_ba, lru_wx=lru_wx, lru_bx=lru_bx, lru_lam=lru_lam,
               da_lq1=da_lq1, da_lk1=da_lk1, da_lq2=da_lq2, da_lk2=da_lk2, da_norm=da_norm, ple_w=ple_w,
               ple_gate_w=ple_gate_w, final_norm=final_norm)
    outs = []
    for x, p in ((x_prompt, p_prompt), (x_sample, p_sample)):
        B, S, _ = x.shape
        outs.append(_trunk(x, p, B, S, prm))
    return tuple(outs)
```

```python
import functools
import math

import numpy as np
import jax
import jax.numpy as jnp
from jax import lax
from jax.experimental import pallas as pl
from jax.experimental.pallas import tpu as pltpu

F32 = jnp.float32
BF16 = jnp.bfloat16

D_MODEL = 1024
PLE_DIM = 256
HEADS = 4
HG_W = 256
HG_DK = HG_W // HEADS
LRU_W = 512
LRU_BLOCK = 64
LRU_C = 8.0
DA_W = 256
DA_QK = 32
DA_V = 64
ROPE_THETA = 500000.0
ROPE_DIM = 8
RMS_EPS = 1e-6
LOG2E = 1.4426950408889634
F32_TINY = float(np.finfo(np.float32).tiny)
D_HG_IN = 5 * HG_W
D_LRU_IN = 2 * LRU_W
D_DA_IN = 4 * DA_W

LANES = 128
VMEM_LIMIT = 56 * 1024 * 1024

TM_IN = 1024
TM_OUT = 1024
T_HG = 256
HG_CHUNK = 64
HG_BLK = 16
T_LRU = 256
TQ = 256
TK = 512


def _sigmoid(x):
    return 1.0 / (1.0 + jnp.exp2(x * -LOG2E))


def _cparams(sem):
    return pltpu.CompilerParams(dimension_semantics=sem, vmem_limit_bytes=VMEM_LIMIT)


def _rope(t, cos, s1, s2):
    halves = []
    for j in range(2):
        th = t[:, LANES * j:LANES * (j + 1)]
        up = pltpu.roll(th, LANES - ROPE_DIM // 2, 1)
        dn = pltpu.roll(th, ROPE_DIM // 2, 1)
        halves.append(th * cos + up * s1 + dn * s2)
    return jnp.concatenate(halves, axis=1)


def _inproj_kernel(x_ref, g_ref, whg_ref, wlru_ref, wda_ref, lb_ref, cos_ref, s1_ref, s2_ref,
                   hq_ref, hkf_ref, hkb_ref, hv_ref, hg_ref, lff_ref, lfb_ref, lx_ref, lg_ref,
                   dqT_ref, dk_ref, dvT_ref, dg_ref):
    x = x_ref[...]
    ms = jnp.mean(x * x, axis=-1, keepdims=True)
    hn = (x * lax.rsqrt(ms + RMS_EPS) * g_ref[...]).astype(BF16)

    z = jnp.dot(hn, whg_ref[...], preferred_element_type=F32)
    zq = z[:, 0:HG_W]
    hq_ref[...] = (zq * _sigmoid(zq)).astype(BF16)
    for d, lf_ref, k_ref in ((0, lff_ref, hkf_ref), (1, lfb_ref, hkb_ref)):
        zf = z[:, HG_W * (1 + d):HG_W * (2 + d)]
        lb = lb_ref[d:d + 1, :]
        s = _sigmoid(zf)
        lf_ref[...] = jnp.log2(lb + (1.0 - lb) * s)
        k_ref[...] = ((1.0 - lb) * (1.0 - s)).astype(BF16)
    hv_ref[...] = z[:, 3 * HG_W:4 * HG_W].astype(BF16)
    hg_ref[...] = z[:, 4 * HG_W:5 * HG_W].astype(BF16)

    z = jnp.dot(hn, wlru_ref[...], preferred_element_type=F32)
    lx_ref[...] = z[:, :LRU_W].astype(BF16)
    lg_ref[...] = z[:, LRU_W:].astype(BF16)

    z = jnp.dot(hn, wda_ref[...], preferred_element_type=F32)
    cos, s1, s2 = cos_ref[...], s1_ref[...], s2_ref[...]
    q = _rope(z[:, 0:DA_W], cos, s1, s2) * (DA_QK ** -0.5 * LOG2E)
    k = _rope(z[:, DA_W:2 * DA_W], cos, s1, s2)
    dqT_ref[...] = q.T.astype(BF16)
    dk_ref[...] = k.astype(BF16)
    dvT_ref[...] = z[:, 2 * DA_W:3 * DA_W].T.astype(BF16)
    dg_ref[...] = z[:, 3 * DA_W:4 * DA_W].astype(BF16)


def _inproj(x2d, B, S, g, whg, wlru, wda, lb, cos_t, s1_t, s2_t):
    R = B * S
    tm = min(TM_IN, S)
    nt = S // tm
    row = lambda w: pl.BlockSpec((tm, w), lambda i: (i, 0))
    full = lambda a: pl.BlockSpec(a.shape, lambda i: (0,) * a.ndim, pipeline_mode=pl.Buffered(1))
    tab = pl.BlockSpec((tm, LANES), lambda i: (i % nt, 0))
    tr = pl.BlockSpec((None, DA_W, tm), lambda i: (i // nt, 0, i % nt))
    sds = jax.ShapeDtypeStruct
    out_shape = (
        sds((R, HG_W), BF16), sds((R, HG_W), BF16), sds((R, HG_W), BF16), sds((R, HG_W), BF16), sds((R, HG_W), BF16),
        sds((R, HG_W), F32), sds((R, HG_W), F32),
        sds((R, LRU_W), BF16), sds((R, LRU_W), BF16),
        sds((B, DA_W, S), BF16), sds((R, DA_W), BF16), sds((B, DA_W, S), BF16), sds((R, DA_W), BF16),
    )
    out_specs = (
        row(HG_W), row(HG_W), row(HG_W), row(HG_W), row(HG_W), row(HG_W), row(HG_W),
        row(LRU_W), row(LRU_W),
        tr, row(DA_W), tr, row(DA_W),
    )
    return pl.pallas_call(
        _inproj_kernel,
        grid=(R // tm,),
        in_specs=[row(D_MODEL), full(g), full(whg), full(wlru), full(wda), full(lb), tab, tab, tab],
        out_specs=out_specs,
        out_shape=out_shape,
        compiler_params=_cparams(("parallel",)),
        name="inproj",
    )(x2d, g, whg, wlru, wda, lb, cos_t, s1_t, s2_t)


def _split2(x):
    hi = x.astype(BF16)
    lo = (x - hi.astype(F32)).astype(BF16)
    return hi, lo


def _dot_t(a, b):
    return lax.dot_general(a, b, (((1,), (1,)), ((), ())), preferred_element_type=F32)


def _dot_a(a, b):
    return lax.dot_general(a, b, (((0,), (0,)), ((), ())), preferred_element_type=F32)


def _head_rows(x_bf, hm_ref):
    zeros = jnp.zeros((HG_BLK, LANES), BF16)
    rows = []
    for h in range(HEADS):
        half = x_bf[:, LANES * (h // 2):LANES * (h // 2 + 1)] * hm_ref[h % 2]
        rows.append(jnp.concatenate([half, zeros] if h < 2 else [zeros, half], axis=1))
    return jnp.concatenate(rows, axis=0)


def _hgrn_tile(q_ref, lf_ref, k_ref, v_ref, ltri_ref, esel_ref, gmask_ref, hm_ref, bd_ref, s_ref, o_ref, reverse):
    T = T_HG
    nblk = HG_CHUNK // HG_BLK
    q = q_ref[...].astype(F32)
    k = k_ref[...].astype(F32)
    v_bf = v_ref[...]
    lf = lf_ref[...]

    ltri = ltri_ref[...]
    c = sum(jnp.dot(ltri, piece, preferred_element_type=F32) for piece in _split2(lf))

    nb = T // HG_BLK
    half = HG_BLK // 2
    c4 = c.reshape(nb, 2, half, HG_W)
    ck4 = (c - jnp.log2(k)).reshape(nb, 2, half, HG_W)
    q4 = q.reshape(nb, 2, half, HG_W)
    far = 0 if reverse else 1
    dacc = jnp.zeros((T, HG_W), F32)
    dfar = jnp.zeros((T // 2, HG_W), F32)
    for jl in range(HG_BLK):
        cj = ck4[:, jl // half, jl % half:jl % half + 1, :][:, None]
        if jl // half == far:
            xj = q4[:, far:far + 1] * jnp.exp2(jnp.minimum(c4[:, far:far + 1] - cj, 0.0))
            dfar = dfar + jnp.dot(xj.reshape(T // 2, HG_W).astype(BF16), esel_ref[jl], preferred_element_type=F32)
        else:
            near = 1 - far
            x_near = q4[:, near:near + 1] * jnp.exp2(jnp.minimum(c4[:, near:near + 1] - cj, 0.0))
            x_far = q4[:, far:far + 1] * jnp.exp2(c4[:, far:far + 1] - cj)
            xj = jnp.concatenate([x_near, x_far] if far else [x_far, x_near], axis=1)
            dacc = dacc + jnp.dot(xj.reshape(T, HG_W).astype(BF16), esel_ref[jl], preferred_element_type=F32)
    d4 = dacc.reshape(nb, 2, half, HG_W)
    dfar4 = d4[:, far:far + 1] + dfar.reshape(nb, 1, half, HG_W)
    d4 = jnp.concatenate([d4[:, 0:1], dfar4] if far else [dfar4, d4[:, 1:2]], axis=1)
    dacc = d4.reshape(T, HG_W) * gmask_ref[...]
    yield

    bd = bd_ref[...]
    zpad64 = jnp.zeros((HG_CHUNK, HG_W), BF16)
    zcol = jnp.zeros((HG_BLK, LANES), F32)
    chunks = list(range(T // HG_CHUNK))
    per_chunk = {}
    for n in chunks:
        r0 = n * HG_CHUNK
        cn = c[r0:r0 + HG_CHUNK]
        qn = q[r0:r0 + HG_CHUNK]
        kn = k[r0:r0 + HG_CHUNK]
        vn = v_bf[r0:r0 + HG_CHUNK]
        ctot = cn[0:1] if reverse else cn[HG_CHUNK - 1:HG_CHUNK]
        qt = (qn * jnp.exp2(cn)).astype(BF16)
        kt = (kn * jnp.exp2(ctot - cn)).astype(BF16)
        upd = bd * _dot_a(vn, kt)
        vbd = jnp.concatenate([_head_rows(vn[HG_BLK * b:HG_BLK * (b + 1)], hm_ref) for b in range(nblk)], axis=0)
        a_rows = []
        for i in range(nblk):
            rows = slice(HG_BLK * i, HG_BLK * (i + 1))
            d_i = dacc[r0 + HG_BLK * i:r0 + HG_BLK * (i + 1)]
            others = list(range(i + 1, nblk)) if reverse else list(range(i))
            if not others:
                a_rows.append(d_i)
                continue
            ref_row = cn[HG_BLK * (i + 1):HG_BLK * (i + 1) + 1] if reverse else cn[HG_BLK * i - 1:HG_BLK * i]
            qh = (qn[rows] * jnp.exp2(cn[rows] - ref_row)).astype(BF16)
            pieces = []
            for b in others:
                brow = slice(HG_BLK * b, HG_BLK * (b + 1))
                kh = (kn[brow] * jnp.exp2(ref_row - cn[brow])).astype(BF16)
                pieces.append(_head_rows(kh, hm_ref))
            if len(pieces) % 2:
                pieces = ([zpad64] + pieces) if reverse else (pieces + [zpad64])
            r_i = _dot_t(qh, jnp.concatenate(pieces, axis=0))
            if r_i.shape[1] == LANES:
                r_i = jnp.concatenate([zcol, r_i] if reverse else [r_i, zcol], axis=1)
            a_rows.append(r_i + d_i)
        a_t = jnp.concatenate(a_rows, axis=0).astype(BF16)
        per_chunk[n] = (qt, jnp.exp2(ctot), upd, a_t, vbd)
    yield

    o_intra = {n: jnp.dot(per_chunk[n][3], per_chunk[n][4], preferred_element_type=F32) for n in chunks}
    s_t = s_ref[...]
    for n in (reversed(chunks) if reverse else chunks):
        qt, decay, upd, _, _ = per_chunk[n]
        r0 = n * HG_CHUNK
        o_ref[r0:r0 + HG_CHUNK, :] = _dot_t(qt, s_t.astype(BF16)) + o_intra[n]
        s_t = s_t * decay + upd
    s_ref[...] = s_t
    yield


def _hgrn_kernel(qf_ref, lff_ref, kf_ref, vf_ref, qb_ref, lfb_ref, kb_ref, vb_ref,
                 ltf_ref, ltb_ref, esel_ref, gmf_ref, gmb_ref, hm_ref, bd_ref,
                 of_ref, ob_ref, sf_ref, sb_ref):
    @pl.when(pl.program_id(1) == 0)
    def _():
        sf_ref[...] = jnp.zeros_like(sf_ref)
        sb_ref[...] = jnp.zeros_like(sb_ref)

    tiles = [
        _hgrn_tile(qf_ref, lff_ref, kf_ref, vf_ref, ltf_ref, esel_ref, gmf_ref, hm_ref, bd_ref, sf_ref, of_ref, False),
        _hgrn_tile(qb_ref, lfb_ref, kb_ref, vb_ref, ltb_ref, esel_ref, gmb_ref, hm_ref, bd_ref, sb_ref, ob_ref, True),
    ]
    for _ in range(3):
        for t in tiles:
            next(t)


@functools.lru_cache(maxsize=None)
def _hgrn_consts():
    T = T_HG
    t = np.arange(T)
    same_chunk = (t[:, None] // HG_CHUNK) == (t[None, :] // HG_CHUNK)
    ltf = (same_chunk & (t[None, :] <= t[:, None])).astype(np.float32)
    ltb = (same_chunk & (t[None, :] >= t[:, None])).astype(np.float32)
    lane = np.arange(HG_W)
    col = np.arange(HG_W)
    esel = np.zeros((HG_BLK, HG_W, HG_W), np.float32)
    grp = HEADS * HG_BLK
    for jl in range(HG_BLK):
        esel[jl] = ((lane[:, None] // HG_DK) == ((col[None, :] % grp) // HG_BLK)) & ((col[None, :] % HG_BLK) == jl)
    blk_of_row = (t % HG_CHUNK) // HG_BLK
    own = (col[None, :] // grp) == blk_of_row[:, None]
    jl_col = col[None, :] % HG_BLK
    gmf = (own & (jl_col <= (t % HG_BLK)[:, None])).astype(np.float32)
    gmb = (own & (jl_col >= (t % HG_BLK)[:, None])).astype(np.float32)
    hm = np.stack([(np.arange(LANES) < HG_DK), (np.arange(LANES) >= HG_DK)]).astype(np.float32)
    hm = np.broadcast_to(hm[:, None, :], (2, HG_BLK, LANES)).copy()
    bd = ((lane[:, None] // HG_DK) == (lane[None, :] // HG_DK)).astype(np.float32)
    return ltf, ltb, esel, gmf, gmb, hm, bd


def _hgrn(hq, lff, hkf, hv, lfb, hkb, B, S):
    T = T_HG
    nt = S // T
    ltf, ltb, esel, gmf, gmb, hm, bd = _hgrn_consts()
    consts = (jnp.asarray(ltf, BF16), jnp.asarray(ltb, BF16), jnp.asarray(esel, BF16),
              jnp.asarray(gmf, F32), jnp.asarray(gmb, F32), jnp.asarray(hm, BF16), jnp.asarray(bd, F32))
    r3 = lambda a: a.reshape(B, S, HG_W)
    fwd = pl.BlockSpec((None, T, HG_W), lambda b, i: (b, i, 0))
    bwd = pl.BlockSpec((None, T, HG_W), lambda b, i: (b, nt - 1 - i, 0))
    full = lambda a: pl.BlockSpec(a.shape, lambda b, i: (0,) * a.ndim, pipeline_mode=pl.Buffered(1))
    of, ob = pl.pallas_call(
        _hgrn_kernel,
        grid=(B, nt),
        in_specs=[fwd, fwd, fwd, fwd, bwd, bwd, bwd, bwd] + [full(a) for a in consts],
        out_specs=(fwd, bwd),
        out_shape=(jax.ShapeDtypeStruct((B, S, HG_W), F32), jax.ShapeDtypeStruct((B, S, HG_W), F32)),
        scratch_shapes=[pltpu.VMEM((HG_W, HG_W), F32), pltpu.VMEM((HG_W, HG_W), F32)],
        compiler_params=_cparams(("parallel", "arbitrary")),
        name="hgrn",
    )(r3(hq), r3(lff), r3(hkf), r3(hv), r3(hq), r3(lfb), r3(hkb), r3(hv), *consts)
    return of.reshape(B * S, HG_W), ob.reshape(B * S, HG_W)


LRU_HALF = LRU_W // 2
LRU_HALO = 16


SUBLANES = 8
LRU_SEG = 16
LRU_NSEG = T_LRU // LRU_SEG
LRU_PITCH = LRU_SEG + SUBLANES


def _scan_chunk(a, b, carry, reverse, a_s, b_s, h_s):
    T, C = a.shape
    L, P, G = LRU_SEG, LRU_PITCH, LRU_NSEG // SUBLANES
    outs, carries = [], []
    for slab in range(C // LANES):
        lanes = slice(LANES * slab, LANES * (slab + 1))
        for s in range(LRU_NSEG):
            a_s[slab, P * s:P * s + L, :] = a[L * s:L * (s + 1), lanes]
            b_s[slab, P * s:P * s + L, :] = b[L * s:L * (s + 1), lanes]
        h = [jnp.zeros((SUBLANES, LANES), F32)] * G
        p = [jnp.ones((SUBLANES, LANES), F32)] * G
        h_loc = [[None] * L for _ in range(G)]
        p_loc = [[None] * L for _ in range(G)]
        for v in (range(L - 1, -1, -1) if reverse else range(L)):
            for g in range(G):
                rows = pl.ds(P * SUBLANES * g + v, SUBLANES, stride=P)
                av = a_s[slab, rows, :]
                h[g] = av * h[g] + b_s[slab, rows, :]
                p[g] = av * p[g]
                h_loc[g][v], p_loc[g][v] = h[g], p[g]
        c = carry[:, lanes]
        inflow = [None] * LRU_NSEG
        for s in (range(LRU_NSEG - 1, -1, -1) if reverse else range(LRU_NSEG)):
            g, r = divmod(s, SUBLANES)
            inflow[s] = c
            c = p[g][r:r + 1] * c + h[g][r:r + 1]
        for g in range(G):
            cin = jnp.concatenate(inflow[SUBLANES * g:SUBLANES * (g + 1)], axis=0)
            for v in range(L):
                h_s[slab, pl.ds(P * SUBLANES * g + v, SUBLANES, stride=P), :] = h_loc[g][v] + p_loc[g][v] * cin
        outs.append(jnp.concatenate([h_s[slab, P * s:P * s + L, :] for s in range(LRU_NSEG)], axis=0))
        carries.append(c)
    return jnp.concatenate(outs, axis=1), jnp.concatenate(carries, axis=1)


def _lru_kernel(lx_ref, lg_ref, cw_ref, cb_ref, w_ref, bias_ref, cneg_ref, out_ref, u_ref, ub_ref, hf_ref, hb_ref,
                *stage_refs, S):
    T = T_LRU
    n = S // T
    C = LRU_HALF
    cw = cw_ref[...]
    cb = cb_ref[...]

    def conv_body(i, _):
        t0 = pl.multiple_of(i * T, T)
        cur = lx_ref[pl.ds(t0, T), :].astype(F32)
        ps = pl.multiple_of(jnp.maximum(t0 - LRU_HALO, 0), LRU_HALO)
        prev = jnp.where(i > 0, lx_ref[pl.ds(ps, LRU_HALO), :].astype(F32), 0.0)
        ns = pl.multiple_of(jnp.minimum(t0 + T, S - LRU_HALO), LRU_HALO)
        nxt = jnp.where(i < n - 1, lx_ref[pl.ds(ns, LRU_HALO), :].astype(F32), 0.0)
        ext = jnp.concatenate([prev, cur, nxt], axis=0)
        h = LRU_HALO
        u = (cb + ext[h - 2:h - 2 + T] * cw[0:1] + ext[h - 1:h - 1 + T] * cw[1:2]
             + cur * cw[2:3] + ext[h + 1:h + 1 + T] * cw[3:4])
        u_ref[pl.ds(t0, T), :] = u
        ub_ref[pl.ds(t0, T), :] = u.astype(BF16)
        return 0

    lax.fori_loop(0, n, conv_body, 0)

    def gates(d, i):
        rows = pl.ds(pl.multiple_of(i * T, T), T)
        u = u_ref[rows, :]
        pre = jnp.dot(ub_ref[rows, :], w_ref[:, 2 * C * d:2 * C * (d + 1)], preferred_element_type=F32)
        r = _sigmoid(pre[:, :C] + bias_ref[2 * d:2 * d + 1, :])
        ig = _sigmoid(pre[:, C:] + bias_ref[2 * d + 1:2 * d + 2, :])
        a = jnp.exp2(cneg_ref[d:d + 1, :] * r)
        y = 1.0 - a * a
        b = y * lax.rsqrt(jnp.maximum(y, F32_TINY)) * (ig * u)
        return a, b

    def emit(rows, h):
        g = lg_ref[rows, :].astype(F32)
        out_ref[rows, :] = (h * (g * _sigmoid(g))).astype(BF16)

    def step(i, carry, meet):
        cf, cr = carry
        j = n - 1 - i
        rows_i = pl.ds(pl.multiple_of(i * T, T), T)
        rows_j = pl.ds(pl.multiple_of(j * T, T), T)
        af, bf = gates(0, i)
        ar, br = gates(1, j)
        hf, cf = _scan_chunk(af, bf, cf, False, *stage_refs[0:3])
        hr, cr = _scan_chunk(ar, br, cr, True, *stage_refs[3:6])
        if meet:
            emit(rows_i, hf + hb_ref[rows_i, :])
            emit(rows_j, hf_ref[rows_j, :] + hr)
        else:
            hf_ref[rows_i, :] = hf
            hb_ref[rows_j, :] = hr
        return cf, cr

    zero = jnp.zeros((1, C), F32)
    unroll = 2 if n % 4 == 0 else 1
    carry = lax.fori_loop(0, n // 2, functools.partial(step, meet=False), (zero, zero), unroll=unroll)
    lax.fori_loop(n // 2, n, functools.partial(step, meet=True), carry, unroll=unroll)


def _lru(lx, lg, B, S, cw, cb, w, bias, cneg):
    C = LRU_HALF
    seq = pl.BlockSpec((None, S, C), lambda b, c: (b, 0, c))
    vec = lambda r: pl.BlockSpec((r, C), lambda b, c: (0, c))
    out = pl.pallas_call(
        functools.partial(_lru_kernel, S=S),
        grid=(B, LRU_W // C),
        in_specs=[seq, seq, vec(4), vec(1), pl.BlockSpec((None, C, 4 * C), lambda b, c: (c, 0, 0)), vec(4), vec(2)],
        out_specs=seq,
        out_shape=jax.ShapeDtypeStruct((B, S, LRU_W), BF16),
        scratch_shapes=[pltpu.VMEM((S, C), F32), pltpu.VMEM((S, C), BF16), pltpu.VMEM((S, C), F32), pltpu.VMEM((S, C), F32)]
        + [pltpu.VMEM((C // LANES, LRU_NSEG * LRU_PITCH, LANES), F32)] * 6,
        compiler_params=_cparams(("parallel", "parallel")),
        name="lru",
    )(lx.reshape(B, S, LRU_W), lg.reshape(B, S, LRU_W), cw, cb, w, bias, cneg)
    return out.reshape(B * S, LRU_W)


PV_ROWS = DA_V + 16
M_INIT = -1e30
ATTN_GROUP = 2


def _attn_kernel(lam_ref, qT_ref, qTn_ref, k_ref, vT_ref, dg_ref, ng_ref, out_ref, rhs_ref, m_ref, acc_ref, sa_ref, sb_ref,
                 *, S, tk, lam_init):
    nk = S // tk
    qi = pl.program_id(1)
    cur_q = qi % 2
    nxt_q = 1 - cur_q
    head_of_row = lax.broadcasted_iota(jnp.int32, (DA_W, TQ), 0) // DA_QK
    ones = jnp.ones((PV_ROWS - DA_V, tk), BF16)

    def make_rhs(q_ref, slot):
        qT = q_ref[...].astype(F32)
        for hc in range(2 * HEADS):
            rhs_ref[slot, hc] = jnp.where(head_of_row == hc, qT, 0.0).astype(BF16)

    def stage(j_next, q_slot, next_ref, j_cur, cur_ref):
        k = k_ref[pl.ds(pl.multiple_of(j_next * tk, tk), tk), :]
        if j_cur is not None:
            vT = vT_ref[:, pl.ds(pl.multiple_of(j_cur * tk, tk), tk)]
        for hc in range(2 * HEADS):
            if hc % ATTN_GROUP == 0:
                for g in range(hc, hc + ATTN_GROUP):
                    next_ref[g] = jnp.dot(k, rhs_ref[q_slot, g], preferred_element_type=F32)
            if j_cur is not None:
                h = hc // 2
                s = cur_ref[hc]
                m_old = m_ref[hc]
                m_new = jnp.maximum(m_old, jnp.max(s, axis=0, keepdims=True))
                p = jnp.exp2(s - m_new).astype(BF16)
                lhs = jnp.concatenate([vT[DA_V * h:DA_V * (h + 1), :], ones], axis=0)
                acc_ref[hc] = jnp.exp2(m_old - m_new) * acc_ref[hc] + jnp.dot(lhs, p, preferred_element_type=F32)
                m_ref[hc] = m_new

    @pl.when(qi == 0)
    def _():
        make_rhs(qT_ref, 0)
        stage(0, 0, sa_ref, None, None)

    make_rhs(qTn_ref, nxt_q)
    m_ref[...] = jnp.full(m_ref.shape, M_INIT, F32)
    acc_ref[...] = jnp.zeros_like(acc_ref)

    def body(i, _):
        stage(2 * i + 1, cur_q, sb_ref, 2 * i, sa_ref)
        stage(2 * i + 2, cur_q, sa_ref, 2 * i + 1, sb_ref)
        return 0

    lax.fori_loop(0, nk // 2 - 1, body, 0)
    stage(nk - 1, cur_q, sb_ref, nk - 2, sa_ref)
    stage(0, nxt_q, sa_ref, nk - 1, sb_ref)

    lam = lam_ref[0]
    outs = []
    for h in range(HEADS):
        a0 = acc_ref[2 * h]
        a1 = acc_ref[2 * h + 1]
        o = a0[:DA_V] * (1.0 / a0[DA_V:DA_V + 1]) - lam * (a1[:DA_V] * (1.0 / a1[DA_V:DA_V + 1]))
        ms = jnp.mean(o * o, axis=0, keepdims=True)
        outs.append(o * lax.rsqrt(ms + RMS_EPS))
    o = jnp.concatenate(outs, axis=0).T
    g = dg_ref[...].astype(F32)
    out_ref[...] = (o * ng_ref[...] * (1.0 - lam_init) * (g * _sigmoid(g))).astype(BF16)


def _attn(lam, dqT, dk, dvT, dg, ng, B, S, lam_init):
    nq = S // TQ
    tk = min(TK, S // 2)
    out = pl.pallas_call(
        functools.partial(_attn_kernel, S=S, tk=tk, lam_init=lam_init),
        grid=(B, nq),
        in_specs=[
            pl.BlockSpec(memory_space=pltpu.SMEM),
            pl.BlockSpec((None, DA_W, TQ), lambda b, i: (b, 0, i)),
            pl.BlockSpec((None, DA_W, TQ), lambda b, i: (b, 0, jnp.minimum(i + 1, nq - 1))),
            pl.BlockSpec((None, S, DA_W), lambda b, i: (b, 0, 0)),
            pl.BlockSpec((None, DA_W, S), lambda b, i: (b, 0, 0)),
            pl.BlockSpec((None, TQ, DA_W), lambda b, i: (b, i, 0)),
            pl.BlockSpec((1, DA_W), lambda b, i: (0, 0)),
        ],
        out_specs=pl.BlockSpec((None, TQ, DA_W), lambda b, i: (b, i, 0)),
        out_shape=jax.ShapeDtypeStruct((B, S, DA_W), BF16),
        scratch_shapes=[
            pltpu.VMEM((2, 2 * HEADS, DA_W, TQ), BF16),
            pltpu.VMEM((2 * HEADS, 1, TQ), F32),
            pltpu.VMEM((2 * HEADS, PV_ROWS, TQ), F32),
            pltpu.VMEM((2 * HEADS, tk, TQ), F32),
            pltpu.VMEM((2 * HEADS, tk, TQ), F32),
        ],
        compiler_params=_cparams(("parallel", "arbitrary")),
        name="attn",
    )(lam, dqT, dqT, dk.reshape(B, S, DA_W), dvT, dg.reshape(B, S, DA_W), ng)
    return out.reshape(B * S, DA_W)


def _outproj_kernel(h_ref, of_ref, ob_ref, hg_ref, olru_ref, oda_ref, p_ref, hgn_ref, ones_ref,
                    wout_ref, gw_ref, pw_ref, fn_ref, out_ref, *, final):
    o = of_ref[...] + ob_ref[...]
    sq = o * o
    hi = sq.astype(BF16)
    lo = (sq - hi.astype(F32)).astype(BF16)
    ones = ones_ref[...]
    ssum = jnp.dot(hi, ones, preferred_element_type=F32) + jnp.dot(lo, ones, preferred_element_type=F32)
    g = hg_ref[...].astype(F32)
    ohg = (o * lax.rsqrt(ssum * (1.0 / HG_DK) + RMS_EPS) * hgn_ref[...] * (g * _sigmoid(g))).astype(BF16)

    acc = h_ref[...]
    acc = acc + jnp.dot(ohg, wout_ref[0:HG_W, :], preferred_element_type=F32)
    acc = acc + jnp.dot(olru_ref[...], wout_ref[HG_W:HG_W + LRU_W, :], preferred_element_type=F32)
    acc = acc + jnp.dot(oda_ref[...], wout_ref[HG_W + LRU_W:, :], preferred_element_type=F32)
    gate = _sigmoid(jnp.dot(acc.astype(BF16), gw_ref[...], preferred_element_type=F32))
    h2 = acc + gate * jnp.dot(p_ref[...].astype(BF16), pw_ref[...], preferred_element_type=F32)
    if final:
        ms = jnp.mean(h2 * h2, axis=-1, keepdims=True)
        h2 = h2 * lax.rsqrt(ms + RMS_EPS) * fn_ref[...]
    out_ref[...] = h2


def _outproj(h, of, ob, hg, olru, oda, p, hgn, ones, wout, gw, pw, fn, final):
    R = h.shape[0]
    tm = min(TM_OUT, R)
    row = lambda w: pl.BlockSpec((tm, w), lambda i: (i, 0))
    full = lambda a: pl.BlockSpec(a.shape, lambda i: (0,) * a.ndim, pipeline_mode=pl.Buffered(1))
    return pl.pallas_call(
        functools.partial(_outproj_kernel, final=final),
        grid=(R // tm,),
        in_specs=[row(D_MODEL), row(HG_W), row(HG_W), row(HG_W), row(LRU_W), row(DA_W), row(PLE_DIM),
                  full(hgn), full(ones), full(wout), full(gw), full(pw), full(fn)],
        out_specs=row(D_MODEL),
        out_shape=jax.ShapeDtypeStruct((R, D_MODEL), F32),
        compiler_params=_cparams(("parallel",)),
        name="outproj",
    )(h, of, ob, hg, olru, oda, p, hgn, ones, wout, gw, pw, fn)


def _rope_tables(S):
    half = ROPE_DIM // 2
    inv = ROPE_THETA ** (-jnp.arange(half, dtype=F32) * 2.0 / ROPE_DIM)
    ang = jnp.arange(S, dtype=F32)[:, None] * inv[None, :]
    cos, sin = jnp.cos(ang), jnp.sin(ang)
    ones = jnp.ones((S, DA_QK - ROPE_DIM), F32)
    zeros = jnp.zeros((S, DA_QK - ROPE_DIM), F32)
    z4 = jnp.zeros((S, half), F32)
    rep = lambda t: jnp.tile(t, (1, LANES // DA_QK))
    return (rep(jnp.concatenate([cos, cos, ones], axis=1)),
            rep(jnp.concatenate([-sin, z4, zeros], axis=1)),
            rep(jnp.concatenate([z4, sin, zeros], axis=1)))


def _lru_gate_weights(wa, wx):
    per_half = LRU_HALF // LRU_BLOCK
    halves = []
    for c in range(LRU_W // LRU_HALF):
        tiles = []
        for d in range(2):
            for w in (wa, wx):
                blocks = [w[d, per_half * c + m] for m in range(per_half)]
                tiles.append(jax.scipy.linalg.block_diag(*blocks))
        halves.append(jnp.concatenate(tiles, axis=1))
    return jnp.stack(halves).astype(BF16)


def _trunk(x, p, B, S, prm):
    R = B * S
    h = x.reshape(R, D_MODEL)
    depth = prm["w_in"].shape[0]
    cos_t, s1_t, s2_t = _rope_tables(S)
    lb_all = jnp.cumsum(jax.nn.softmax(prm["hg_lb"].astype(F32), axis=0), axis=0)
    lb_all = lb_all - lb_all[:1]
    lane = np.arange(HG_W)
    ones_bd = jnp.asarray((lane[:, None] // HG_DK) == (lane[None, :] // HG_DK), BF16)
    for l in range(depth):
        w_in = prm["w_in"][l].astype(BF16)
        (hq, hkf, hkb, hv, hg, lff, lfb, lx, lg, dqT, dk, dvT, dg) = _inproj(
            h, B, S, prm["norm_g"][l][None, :], w_in[:, :D_HG_IN], w_in[:, D_HG_IN:D_HG_IN + D_LRU_IN],
            w_in[:, D_HG_IN + D_LRU_IN:], lb_all[l], cos_t, s1_t, s2_t)

        of, ob = _hgrn(hq, lff, hkf, hv, lfb, hkb, B, S)

        cneg = -LRU_C * LOG2E * jax.nn.softplus(-prm["lru_lam"][l].astype(F32))
        bias = jnp.stack([prm["lru_ba"][l, 0], prm["lru_bx"][l, 0], prm["lru_ba"][l, 1], prm["lru_bx"][l, 1]])
        olru = _lru(lx, lg, B, S, prm["lru_conv_w"][l], prm["lru_conv_b"][l][None, :],
                    _lru_gate_weights(prm["lru_wa"][l], prm["lru_wx"][l]), bias, cneg)

        lam_init = 0.8 - 0.6 * math.exp(-0.3 * l)
        lam = (jnp.exp(jnp.sum(prm["da_lq1"][l].astype(F32) * prm["da_lk1"][l].astype(F32)))
               - jnp.exp(jnp.sum(prm["da_lq2"][l].astype(F32) * prm["da_lk2"][l].astype(F32))) + lam_init)
        oda = _attn(lam.reshape(1), dqT, dk, dvT, dg, jnp.tile(prm["da_norm"][l], HEADS)[None, :], B, S, lam_init)

        h = _outproj(h, of, ob, hg, olru, oda, p[l].reshape(R, PLE_DIM), jnp.tile(prm["hg_norm"][l], HEADS)[None, :],
                     ones_bd, prm["w_out"][l].astype(BF16), prm["ple_gate_w"][l].astype(BF16),
                     prm["ple_w"][l].astype(BF16), prm["final_norm"][None, :], final=(l == depth - 1))
    return h.reshape(B, S, D_MODEL)


def kernel(x_prompt, x_sample, p_prompt, p_sample, norm_g, w_in, w_out, hg_lb, hg_norm, lru_conv_w, lru_conv_b,
           lru_wa, lru_ba, lru_wx, lru_bx, lru_lam, da_lq1, da_lk1, da_lq2, da_lk2, da_norm, ple_w, ple_gate_w,
           final_norm):
    prm = dict(norm_g=norm_g, w_in=w_in, w_out=w_out, hg_lb=hg_lb, hg_norm=hg_norm, lru_conv_w=lru_conv_w,
               lru_conv_b=lru_conv_b, lru_wa=lru_wa, lru_ba=lru_ba, lru_wx=lru_wx, lru_bx=lru_bx, lru_lam=lru_lam,
               da_lq1=da_lq1, da_lk1=da_lk1, da_lq2=da_lq2, da_lk2=da_lk2, da_norm=da_norm, ple_w=ple_w,
               ple_gate_w=ple_gate_w, final_norm=final_norm)
    outs = []
    for x, p in ((x_prompt, p_prompt), (x_sample, p_sample)):
        B, S, _ = x.shape
        outs.append(_trunk(x, p, B, S, prm))
    return tuple(outs)
```

```python
import functools
import math

import numpy as np
import jax
import jax.numpy as jnp
from jax import lax
from jax.experimental import pallas as pl
from jax.experimental.pallas import tpu as pltpu

F32 = jnp.float32
BF16 = jnp.bfloat16

D_MODEL = 1024
PLE_DIM = 256
HEADS = 4
HG_W = 256
HG_DK = HG_W // HEADS
LRU_W = 512
LRU_BLOCK = 64
LRU_C = 8.0
DA_W = 256
DA_QK = 32
DA_V = 64
ROPE_THETA = 500000.0
ROPE_DIM = 8
RMS_EPS = 1e-6
LOG2E = 1.4426950408889634
F32_TINY = float(np.finfo(np.float32).tiny)
D_HG_IN = 5 * HG_W
D_LRU_IN = 2 * LRU_W
D_DA_IN = 4 * DA_W

LANES = 128
VMEM_LIMIT = 56 * 1024 * 1024

TM_IN = 1024
TM_OUT = 1024
T_HG = 256
HG_CHUNK = 64
HG_BLK = 16
T_LRU = 256
TQ = 256
TK = 512


def _sigmoid(x):
    return 1.0 / (1.0 + jnp.exp2(x * -LOG2E))


def _cparams(sem):
    return pltpu.CompilerParams(dimension_semantics=sem, vmem_limit_bytes=VMEM_LIMIT)


def _rope(t, cos, s1, s2):
    halves = []
    for j in range(2):
        th = t[:, LANES * j:LANES * (j + 1)]
        up = pltpu.roll(th, LANES - ROPE_DIM // 2, 1)
        dn = pltpu.roll(th, ROPE_DIM // 2, 1)
        halves.append(th * cos + up * s1 + dn * s2)
    return jnp.concatenate(halves, axis=1)


def _inproj_kernel(x_ref, g_ref, w_ref, lb_ref, cos_ref, s1_ref, s2_ref,
                   hq_ref, hkf_ref, hkb_ref, hv_ref, hg_ref, lff_ref, lfb_ref, lx_ref, lg_ref,
                   dqT_ref, dk_ref, dvT_ref, dg_ref):
    x = x_ref[...]
    ms = jnp.mean(x * x, axis=-1, keepdims=True)
    hn = (x * lax.rsqrt(ms + RMS_EPS) * g_ref[...]).astype(BF16)

    z = jnp.dot(hn, w_ref[:, 0:D_HG_IN], preferred_element_type=F32)
    zq = z[:, 0:HG_W]
    hq_ref[...] = (zq * _sigmoid(zq)).astype(BF16)
    for d, lf_ref, k_ref in ((0, lff_ref, hkf_ref), (1, lfb_ref, hkb_ref)):
        zf = z[:, HG_W * (1 + d):HG_W * (2 + d)]
        lb = lb_ref[d:d + 1, :]
        s = _sigmoid(zf)
        lf_ref[...] = jnp.log2(lb + (1.0 - lb) * s)
        k_ref[...] = ((1.0 - lb) * (1.0 - s)).astype(BF16)
    hv_ref[...] = z[:, 3 * HG_W:4 * HG_W].astype(BF16)
    hg_ref[...] = z[:, 4 * HG_W:5 * HG_W].astype(BF16)

    z = jnp.dot(hn, w_ref[:, D_HG_IN:D_HG_IN + D_LRU_IN], preferred_element_type=F32)
    lx_ref[...] = z[:, :LRU_W].astype(BF16)
    lg_ref[...] = z[:, LRU_W:].astype(BF16)

    z = jnp.dot(hn, w_ref[:, D_HG_IN + D_LRU_IN:], preferred_element_type=F32)
    cos, s1, s2 = cos_ref[...], s1_ref[...], s2_ref[...]
    q = _rope(z[:, 0:DA_W], cos, s1, s2) * (DA_QK ** -0.5 * LOG2E)
    k = _rope(z[:, DA_W:2 * DA_W], cos, s1, s2)
    dqT_ref[...] = q.T.astype(BF16)
    dk_ref[...] = k.astype(BF16)
    dvT_ref[...] = z[:, 2 * DA_W:3 * DA_W].T.astype(BF16)
    dg_ref[...] = z[:, 3 * DA_W:4 * DA_W].astype(BF16)


def _layer_block(a, l):
    return pl.BlockSpec((None,) + a.shape[1:], lambda i: (l,) + (0,) * (a.ndim - 1), pipeline_mode=pl.Buffered(1))


def _inproj(x2d, B, S, l, g, w_in, lb, cos_t, s1_t, s2_t):
    R = B * S
    tm = min(TM_IN, S)
    nt = S // tm
    row = lambda w: pl.BlockSpec((tm, w), lambda i: (i, 0))
    full = lambda a: pl.BlockSpec(a.shape, lambda i: (0,) * a.ndim, pipeline_mode=pl.Buffered(1))
    tab = pl.BlockSpec((tm, LANES), lambda i: (i % nt, 0))
    tr = pl.BlockSpec((None, DA_W, tm), lambda i: (i // nt, 0, i % nt))
    sds = jax.ShapeDtypeStruct
    out_shape = (
        sds((R, HG_W), BF16), sds((R, HG_W), BF16), sds((R, HG_W), BF16), sds((R, HG_W), BF16), sds((R, HG_W), BF16),
        sds((R, HG_W), F32), sds((R, HG_W), F32),
        sds((R, LRU_W), BF16), sds((R, LRU_W), BF16),
        sds((B, DA_W, S), BF16), sds((R, DA_W), BF16), sds((B, DA_W, S), BF16), sds((R, DA_W), BF16),
    )
    out_specs = (
        row(HG_W), row(HG_W), row(HG_W), row(HG_W), row(HG_W), row(HG_W), row(HG_W),
        row(LRU_W), row(LRU_W),
        tr, row(DA_W), tr, row(DA_W),
    )
    return pl.pallas_call(
        _inproj_kernel,
        grid=(R // tm,),
        in_specs=[row(D_MODEL), full(g), _layer_block(w_in, l), full(lb), tab, tab, tab],
        out_specs=out_specs,
        out_shape=out_shape,
        compiler_params=_cparams(("parallel",)),
        name="inproj",
    )(x2d, g, w_in, lb, cos_t, s1_t, s2_t)


def _split2(x):
    hi = x.astype(BF16)
    lo = (x - hi.astype(F32)).astype(BF16)
    return hi, lo


def _dot_t(a, b):
    return lax.dot_general(a, b, (((1,), (1,)), ((), ())), preferred_element_type=F32)


def _dot_a(a, b):
    return lax.dot_general(a, b, (((0,), (0,)), ((), ())), preferred_element_type=F32)


def _head_rows(x_bf, hm_ref):
    zeros = jnp.zeros((HG_BLK, LANES), BF16)
    rows = []
    for h in range(HEADS):
        half = x_bf[:, LANES * (h // 2):LANES * (h // 2 + 1)] * hm_ref[h % 2]
        rows.append(jnp.concatenate([half, zeros] if h < 2 else [zeros, half], axis=1))
    return jnp.concatenate(rows, axis=0)


def _hgrn_tile(q_ref, lf_ref, k_ref, v_ref, ltri_ref, esel_ref, gmask_ref, hm_ref, bd_ref, s_ref, o_ref, reverse):
    T = T_HG
    nblk = HG_CHUNK // HG_BLK
    q = q_ref[...].astype(F32)
    k = k_ref[...].astype(F32)
    v_bf = v_ref[...]
    lf = lf_ref[...]

    ltri = ltri_ref[...]
    c = sum(jnp.dot(ltri, piece, preferred_element_type=F32) for piece in _split2(lf))

    nb = T // HG_BLK
    half = HG_BLK // 2
    c4 = c.reshape(nb, 2, half, HG_W)
    ck4 = (c - jnp.log2(k)).reshape(nb, 2, half, HG_W)
    q4 = q.reshape(nb, 2, half, HG_W)
    far = 0 if reverse else 1
    dacc = jnp.zeros((T, HG_W), F32)
    dfar = jnp.zeros((T // 2, HG_W), F32)
    for jl in range(HG_BLK):
        cj = ck4[:, jl // half, jl % half:jl % half + 1, :][:, None]
        if jl // half == far:
            xj = q4[:, far:far + 1] * jnp.exp2(jnp.minimum(c4[:, far:far + 1] - cj, 0.0))
            dfar = dfar + jnp.dot(xj.reshape(T // 2, HG_W).astype(BF16), esel_ref[jl], preferred_element_type=F32)
        else:
            near = 1 - far
            x_near = q4[:, near:near + 1] * jnp.exp2(jnp.minimum(c4[:, near:near + 1] - cj, 0.0))
            x_far = q4[:, far:far + 1] * jnp.exp2(c4[:, far:far + 1] - cj)
            xj = jnp.concatenate([x_near, x_far] if far else [x_far, x_near], axis=1)
            dacc = dacc + jnp.dot(xj.reshape(T, HG_W).astype(BF16), esel_ref[jl], preferred_element_type=F32)
    d4 = dacc.reshape(nb, 2, half, HG_W)
    dfar4 = d4[:, far:far + 1] + dfar.reshape(nb, 1, half, HG_W)
    d4 = jnp.concatenate([d4[:, 0:1], dfar4] if far else [dfar4, d4[:, 1:2]], axis=1)
    dacc = d4.reshape(T, HG_W) * gmask_ref[...]
    yield

    bd = bd_ref[...]
    zpad64 = jnp.zeros((HG_CHUNK, HG_W), BF16)
    zcol = jnp.zeros((HG_BLK, LANES), F32)
    chunks = list(range(T // HG_CHUNK))
    per_chunk = {}
    for n in chunks:
        r0 = n * HG_CHUNK
        cn = c[r0:r0 + HG_CHUNK]
        qn = q[r0:r0 + HG_CHUNK]
        kn = k[r0:r0 + HG_CHUNK]
        vn = v_bf[r0:r0 + HG_CHUNK]
        ctot = cn[0:1] if reverse else cn[HG_CHUNK - 1:HG_CHUNK]
        qt = (qn * jnp.exp2(cn)).astype(BF16)
        kt = (kn * jnp.exp2(ctot - cn)).astype(BF16)
        upd = bd * _dot_a(vn, kt)
        vbd = jnp.concatenate([_head_rows(vn[HG_BLK * b:HG_BLK * (b + 1)], hm_ref) for b in range(nblk)], axis=0)
        a_rows = []
        for i in range(nblk):
            rows = slice(HG_BLK * i, HG_BLK * (i + 1))
            d_i = dacc[r0 + HG_BLK * i:r0 + HG_BLK * (i + 1)]
            others = list(range(i + 1, nblk)) if reverse else list(range(i))
            if not others:
                a_rows.append(d_i)
                continue
            ref_row = cn[HG_BLK * (i + 1):HG_BLK * (i + 1) + 1] if reverse else cn[HG_BLK * i - 1:HG_BLK * i]
            qh = (qn[rows] * jnp.exp2(cn[rows] - ref_row)).astype(BF16)
            pieces = []
            for b in others:
                brow = slice(HG_BLK * b, HG_BLK * (b + 1))
                kh = (kn[brow] * jnp.exp2(ref_row - cn[brow])).astype(BF16)
                pieces.append(_head_rows(kh, hm_ref))
            if len(pieces) % 2:
                pieces = ([zpad64] + pieces) if reverse else (pieces + [zpad64])
            r_i = _dot_t(qh, jnp.concatenate(pieces, axis=0))
            if r_i.shape[1] == LANES:
                r_i = jnp.concatenate([zcol, r_i] if reverse else [r_i, zcol], axis=1)
            a_rows.append(r_i + d_i)
        a_t = jnp.concatenate(a_rows, axis=0).astype(BF16)
        per_chunk[n] = (qt, jnp.exp2(ctot), upd, a_t, vbd)
    yield

    o_intra = {n: jnp.dot(per_chunk[n][3], per_chunk[n][4], preferred_element_type=F32) for n in chunks}
    s_t = s_ref[...]
    for n in (reversed(chunks) if reverse else chunks):
        qt, decay, upd, _, _ = per_chunk[n]
        r0 = n * HG_CHUNK
        o_ref[r0:r0 + HG_CHUNK, :] = _dot_t(qt, s_t.astype(BF16)) + o_intra[n]
        s_t = s_t * decay + upd
    s_ref[...] = s_t
    yield


def _hgrn_kernel(qf_ref, lff_ref, kf_ref, vf_ref, qb_ref, lfb_ref, kb_ref, vb_ref,
                 ltf_ref, ltb_ref, esel_ref, gmf_ref, gmb_ref, hm_ref, bd_ref,
                 of_ref, ob_ref, sf_ref, sb_ref):
    @pl.when(pl.program_id(1) == 0)
    def _():
        sf_ref[...] = jnp.zeros_like(sf_ref)
        sb_ref[...] = jnp.zeros_like(sb_ref)

    tiles = [
        _hgrn_tile(qf_ref, lff_ref, kf_ref, vf_ref, ltf_ref, esel_ref, gmf_ref, hm_ref, bd_ref, sf_ref, of_ref, False),
        _hgrn_tile(qb_ref, lfb_ref, kb_ref, vb_ref, ltb_ref, esel_ref, gmb_ref, hm_ref, bd_ref, sb_ref, ob_ref, True),
    ]
    for _ in range(3):
        for t in tiles:
            next(t)


@functools.lru_cache(maxsize=None)
def _hgrn_consts():
    T = T_HG
    t = np.arange(T)
    same_chunk = (t[:, None] // HG_CHUNK) == (t[None, :] // HG_CHUNK)
    ltf = (same_chunk & (t[None, :] <= t[:, None])).astype(np.float32)
    ltb = (same_chunk & (t[None, :] >= t[:, None])).astype(np.float32)
    lane = np.arange(HG_W)
    col = np.arange(HG_W)
    esel = np.zeros((HG_BLK, HG_W, HG_W), np.float32)
    grp = HEADS * HG_BLK
    for jl in range(HG_BLK):
        esel[jl] = ((lane[:, None] // HG_DK) == ((col[None, :] % grp) // HG_BLK)) & ((col[None, :] % HG_BLK) == jl)
    blk_of_row = (t % HG_CHUNK) // HG_BLK
    own = (col[None, :] // grp) == blk_of_row[:, None]
    jl_col = col[None, :] % HG_BLK
    gmf = (own & (jl_col <= (t % HG_BLK)[:, None])).astype(np.float32)
    gmb = (own & (jl_col >= (t % HG_BLK)[:, None])).astype(np.float32)
    hm = np.stack([(np.arange(LANES) < HG_DK), (np.arange(LANES) >= HG_DK)]).astype(np.float32)
    hm = np.broadcast_to(hm[:, None, :], (2, HG_BLK, LANES)).copy()
    bd = ((lane[:, None] // HG_DK) == (lane[None, :] // HG_DK)).astype(np.float32)
    return ltf, ltb, esel, gmf, gmb, hm, bd


def _hgrn(hq, lff, hkf, hv, lfb, hkb, B, S):
    T = T_HG
    nt = S // T
    ltf, ltb, esel, gmf, gmb, hm, bd = _hgrn_consts()
    consts = (jnp.asarray(ltf, BF16), jnp.asarray(ltb, BF16), jnp.asarray(esel, BF16),
              jnp.asarray(gmf, F32), jnp.asarray(gmb, F32), jnp.asarray(hm, BF16), jnp.asarray(bd, F32))
    r3 = lambda a: a.reshape(B, S, HG_W)
    fwd = pl.BlockSpec((None, T, HG_W), lambda b, i: (b, i, 0))
    bwd = pl.BlockSpec((None, T, HG_W), lambda b, i: (b, nt - 1 - i, 0))
    full = lambda a: pl.BlockSpec(a.shape, lambda b, i: (0,) * a.ndim, pipeline_mode=pl.Buffered(1))
    of, ob = pl.pallas_call(
        _hgrn_kernel,
        grid=(B, nt),
        in_specs=[fwd, fwd, fwd, fwd, bwd, bwd, bwd, bwd] + [full(a) for a in consts],
        out_specs=(fwd, bwd),
        out_shape=(jax.ShapeDtypeStruct((B, S, HG_W), F32), jax.ShapeDtypeStruct((B, S, HG_W), F32)),
        scratch_shapes=[pltpu.VMEM((HG_W, HG_W), F32), pltpu.VMEM((HG_W, HG_W), F32)],
        compiler_params=_cparams(("parallel", "arbitrary")),
        name="hgrn",
    )(r3(hq), r3(lff), r3(hkf), r3(hv), r3(hq), r3(lfb), r3(hkb), r3(hv), *consts)
    return of.reshape(B * S, HG_W), ob.reshape(B * S, HG_W)


LRU_HALF = LRU_W // 2
LRU_HALO = 16


SUBLANES = 8
LRU_SEG = 16
LRU_NSEG = T_LRU // LRU_SEG
LRU_PITCH = LRU_SEG + SUBLANES


def _scan_chunk(a, b, carry, reverse, a_s, b_s, h_s):
    T, C = a.shape
    L, P, G = LRU_SEG, LRU_PITCH, LRU_NSEG // SUBLANES
    outs, carries = [], []
    for slab in range(C // LANES):
        lanes = slice(LANES * slab, LANES * (slab + 1))
        for s in range(LRU_NSEG):
            a_s[slab, P * s:P * s + L, :] = a[L * s:L * (s + 1), lanes]
            b_s[slab, P * s:P * s + L, :] = b[L * s:L * (s + 1), lanes]
        h = [jnp.zeros((SUBLANES, LANES), F32)] * G
        p = [jnp.ones((SUBLANES, LANES), F32)] * G
        h_loc = [[None] * L for _ in range(G)]
        p_loc = [[None] * L for _ in range(G)]
        for v in (range(L - 1, -1, -1) if reverse else range(L)):
            for g in range(G):
                rows = pl.ds(P * SUBLANES * g + v, SUBLANES, stride=P)
                av = a_s[slab, rows, :]
                h[g] = av * h[g] + b_s[slab, rows, :]
                p[g] = av * p[g]
                h_loc[g][v], p_loc[g][v] = h[g], p[g]
        c = carry[:, lanes]
        inflow = [None] * LRU_NSEG
        for s in (range(LRU_NSEG - 1, -1, -1) if reverse else range(LRU_NSEG)):
            g, r = divmod(s, SUBLANES)
            inflow[s] = c
            c = p[g][r:r + 1] * c + h[g][r:r + 1]
        for g in range(G):
            cin = jnp.concatenate(inflow[SUBLANES * g:SUBLANES * (g + 1)], axis=0)
            for v in range(L):
                h_s[slab, pl.ds(P * SUBLANES * g + v, SUBLANES, stride=P), :] = h_loc[g][v] + p_loc[g][v] * cin
        outs.append(jnp.concatenate([h_s[slab, P * s:P * s + L, :] for s in range(LRU_NSEG)], axis=0))
        carries.append(c)
    return jnp.concatenate(outs, axis=1), jnp.concatenate(carries, axis=1)


def _lru_kernel(lx_ref, lg_ref, cw_ref, cb_ref, w_ref, bias_ref, cneg_ref, out_ref, u_ref, ub_ref, hf_ref, hb_ref,
                *stage_refs, S):
    T = T_LRU
    n = S // T
    C = LRU_HALF
    cw = cw_ref[...]
    cb = cb_ref[...]

    def conv_body(i, _):
        t0 = pl.multiple_of(i * T, T)
        cur = lx_ref[pl.ds(t0, T), :].astype(F32)
        ps = pl.multiple_of(jnp.maximum(t0 - LRU_HALO, 0), LRU_HALO)
        prev = jnp.where(i > 0, lx_ref[pl.ds(ps, LRU_HALO), :].astype(F32), 0.0)
        ns = pl.multiple_of(jnp.minimum(t0 + T, S - LRU_HALO), LRU_HALO)
        nxt = jnp.where(i < n - 1, lx_ref[pl.ds(ns, LRU_HALO), :].astype(F32), 0.0)
        ext = jnp.concatenate([prev, cur, nxt], axis=0)
        h = LRU_HALO
        u = (cb + ext[h - 2:h - 2 + T] * cw[0:1] + ext[h - 1:h - 1 + T] * cw[1:2]
             + cur * cw[2:3] + ext[h + 1:h + 1 + T] * cw[3:4])
        u_ref[pl.ds(t0, T), :] = u
        ub_ref[pl.ds(t0, T), :] = u.astype(BF16)
        return 0

    lax.fori_loop(0, n, conv_body, 0)

    def gates(d, i):
        rows = pl.ds(pl.multiple_of(i * T, T), T)
        u = u_ref[rows, :]
        pre = jnp.dot(ub_ref[rows, :], w_ref[:, 2 * C * d:2 * C * (d + 1)], preferred_element_type=F32)
        r = _sigmoid(pre[:, :C] + bias_ref[2 * d:2 * d + 1, :])
        ig = _sigmoid(pre[:, C:] + bias_ref[2 * d + 1:2 * d + 2, :])
        a = jnp.exp2(cneg_ref[d:d + 1, :] * r)
        y = 1.0 - a * a
        b = y * lax.rsqrt(jnp.maximum(y, F32_TINY)) * (ig * u)
        return a, b

    def emit(rows, h):
        g = lg_ref[rows, :].astype(F32)
        out_ref[rows, :] = (h * (g * _sigmoid(g))).astype(BF16)

    def step(i, carry, meet):
        cf, cr = carry
        j = n - 1 - i
        rows_i = pl.ds(pl.multiple_of(i * T, T), T)
        rows_j = pl.ds(pl.multiple_of(j * T, T), T)
        af, bf = gates(0, i)
        ar, br = gates(1, j)
        hf, cf = _scan_chunk(af, bf, cf, False, *stage_refs[0:3])
        hr, cr = _scan_chunk(ar, br, cr, True, *stage_refs[3:6])
        if meet:
            emit(rows_i, hf + hb_ref[rows_i, :])
            emit(rows_j, hf_ref[rows_j, :] + hr)
        else:
            hf_ref[rows_i, :] = hf
            hb_ref[rows_j, :] = hr
        return cf, cr

    zero = jnp.zeros((1, C), F32)
    unroll = 2 if n % 4 == 0 else 1
    carry = lax.fori_loop(0, n // 2, functools.partial(step, meet=False), (zero, zero), unroll=unroll)
    lax.fori_loop(n // 2, n, functools.partial(step, meet=True), carry, unroll=unroll)


def _lru(lx, lg, B, S, cw, cb, w, bias, cneg):
    C = LRU_HALF
    seq = pl.BlockSpec((None, S, C), lambda b, c: (b, 0, c))
    vec = lambda r: pl.BlockSpec((r, C), lambda b, c: (0, c))
    out = pl.pallas_call(
        functools.partial(_lru_kernel, S=S),
        grid=(B, LRU_W // C),
        in_specs=[seq, seq, vec(4), vec(1), pl.BlockSpec((None, C, 4 * C), lambda b, c: (c, 0, 0)), vec(4), vec(2)],
        out_specs=seq,
        out_shape=jax.ShapeDtypeStruct((B, S, LRU_W), BF16),
        scratch_shapes=[pltpu.VMEM((S, C), F32), pltpu.VMEM((S, C), BF16), pltpu.VMEM((S, C), F32), pltpu.VMEM((S, C), F32)]
        + [pltpu.VMEM((C // LANES, LRU_NSEG * LRU_PITCH, LANES), F32)] * 6,
        compiler_params=_cparams(("parallel", "parallel")),
        name="lru",
    )(lx.reshape(B, S, LRU_W), lg.reshape(B, S, LRU_W), cw, cb, w, bias, cneg)
    return out.reshape(B * S, LRU_W)


PV_ROWS = DA_V + 16
M_INIT = -1e30
ATTN_GROUP = 2


def _attn_kernel(lam_ref, qT_ref, qTn_ref, k_ref, vT_ref, dg_ref, ng_ref, out_ref, rhs_ref, m_ref, acc_ref, sa_ref, sb_ref,
                 *, S, tk, lam_init):
    nk = S // tk
    qi = pl.program_id(1)
    cur_q = qi % 2
    nxt_q = 1 - cur_q
    head_of_row = lax.broadcasted_iota(jnp.int32, (DA_W, TQ), 0) // DA_QK
    ones = jnp.ones((PV_ROWS - DA_V, tk), BF16)

    def make_rhs(q_ref, slot):
        qT = q_ref[...].astype(F32)
        for hc in range(2 * HEADS):
            rhs_ref[slot, hc] = jnp.where(head_of_row == hc, qT, 0.0).astype(BF16)

    def stage(j_next, q_slot, next_ref, j_cur, cur_ref):
        k = k_ref[pl.ds(pl.multiple_of(j_next * tk, tk), tk), :]
        if j_cur is not None:
            vT = vT_ref[:, pl.ds(pl.multiple_of(j_cur * tk, tk), tk)]
        for hc in range(2 * HEADS):
            if hc % ATTN_GROUP == 0:
                for g in range(hc, hc + ATTN_GROUP):
                    next_ref[g] = jnp.dot(k, rhs_ref[q_slot, g], preferred_element_type=F32)
            if j_cur is not None:
                h = hc // 2
                s = cur_ref[hc]
                m_old = m_ref[hc]
                m_new = jnp.maximum(m_old, jnp.max(s, axis=0, keepdims=True))
                p = jnp.exp2(s - m_new).astype(BF16)
                lhs = jnp.concatenate([vT[DA_V * h:DA_V * (h + 1), :], ones], axis=0)
                acc_ref[hc] = jnp.exp2(m_old - m_new) * acc_ref[hc] + jnp.dot(lhs, p, preferred_element_type=F32)
                m_ref[hc] = m_new

    @pl.when(qi == 0)
    def _():
        make_rhs(qT_ref, 0)
        stage(0, 0, sa_ref, None, None)

    make_rhs(qTn_ref, nxt_q)
    m_ref[...] = jnp.full(m_ref.shape, M_INIT, F32)
    acc_ref[...] = jnp.zeros_like(acc_ref)

    def body(i, _):
        stage(2 * i + 1, cur_q, sb_ref, 2 * i, sa_ref)
        stage(2 * i + 2, cur_q, sa_ref, 2 * i + 1, sb_ref)
        return 0

    lax.fori_loop(0, nk // 2 - 1, body, 0)
    stage(nk - 1, cur_q, sb_ref, nk - 2, sa_ref)
    stage(0, nxt_q, sa_ref, nk - 1, sb_ref)

    lam = lam_ref[0]
    outs = []
    for h in range(HEADS):
        a0 = acc_ref[2 * h]
        a1 = acc_ref[2 * h + 1]
        o = a0[:DA_V] * (1.0 / a0[DA_V:DA_V + 1]) - lam * (a1[:DA_V] * (1.0 / a1[DA_V:DA_V + 1]))
        ms = jnp.mean(o * o, axis=0, keepdims=True)
        outs.append(o * lax.rsqrt(ms + RMS_EPS))
    o = jnp.concatenate(outs, axis=0).T
    g = dg_ref[...].astype(F32)
    out_ref[...] = (o * ng_ref[...] * (1.0 - lam_init) * (g * _sigmoid(g))).astype(BF16)


def _attn(lam, dqT, dk, dvT, dg, ng, B, S, lam_init):
    nq = S // TQ
    tk = min(TK, S // 2)
    out = pl.pallas_call(
        functools.partial(_attn_kernel, S=S, tk=tk, lam_init=lam_init),
        grid=(B, nq),
        in_specs=[
            pl.BlockSpec(memory_space=pltpu.SMEM),
            pl.BlockSpec((None, DA_W, TQ), lambda b, i: (b, 0, i)),
            pl.BlockSpec((None, DA_W, TQ), lambda b, i: (b, 0, jnp.minimum(i + 1, nq - 1))),
            pl.BlockSpec((None, S, DA_W), lambda b, i: (b, 0, 0)),
            pl.BlockSpec((None, DA_W, S), lambda b, i: (b, 0, 0)),
            pl.BlockSpec((None, TQ, DA_W), lambda b, i: (b, i, 0)),
            pl.BlockSpec((1, DA_W), lambda b, i: (0, 0)),
        ],
        out_specs=pl.BlockSpec((None, TQ, DA_W), lambda b, i: (b, i, 0)),
        out_shape=jax.ShapeDtypeStruct((B, S, DA_W), BF16),
        scratch_shapes=[
            pltpu.VMEM((2, 2 * HEADS, DA_W, TQ), BF16),
            pltpu.VMEM((2 * HEADS, 1, TQ), F32),
            pltpu.VMEM((2 * HEADS, PV_ROWS, TQ), F32),
            pltpu.VMEM((2 * HEADS, tk, TQ), F32),
            pltpu.VMEM((2 * HEADS, tk, TQ), F32),
        ],
        compiler_params=_cparams(("parallel", "arbitrary")),
        name="attn",
    )(lam, dqT, dqT, dk.reshape(B, S, DA_W), dvT, dg.reshape(B, S, DA_W), ng)
    return out.reshape(B * S, DA_W)


def _outproj_kernel(h_ref, of_ref, ob_ref, hg_ref, olru_ref, oda_ref, p_ref, hgn_ref, ones_ref,
                    wout_ref, gw_ref, pw_ref, fn_ref, out_ref, *, final):
    o = of_ref[...] + ob_ref[...]
    sq = o * o
    hi = sq.astype(BF16)
    lo = (sq - hi.astype(F32)).astype(BF16)
    ones = ones_ref[...]
    ssum = jnp.dot(hi, ones, preferred_element_type=F32) + jnp.dot(lo, ones, preferred_element_type=F32)
    g = hg_ref[...].astype(F32)
    ohg = (o * lax.rsqrt(ssum * (1.0 / HG_DK) + RMS_EPS) * hgn_ref[...] * (g * _sigmoid(g))).astype(BF16)

    acc = h_ref[...]
    acc = acc + jnp.dot(ohg, wout_ref[0:HG_W, :], preferred_element_type=F32)
    acc = acc + jnp.dot(olru_ref[...], wout_ref[HG_W:HG_W + LRU_W, :], preferred_element_type=F32)
    acc = acc + jnp.dot(oda_ref[...], wout_ref[HG_W + LRU_W:, :], preferred_element_type=F32)
    gate = _sigmoid(jnp.dot(acc.astype(BF16), gw_ref[...], preferred_element_type=F32))
    h2 = acc + gate * jnp.dot(p_ref[...].astype(BF16), pw_ref[...], preferred_element_type=F32)
    if final:
        ms = jnp.mean(h2 * h2, axis=-1, keepdims=True)
        h2 = h2 * lax.rsqrt(ms + RMS_EPS) * fn_ref[...]
    out_ref[...] = h2


def _outproj(h, of, ob, hg, olru, oda, p, l, hgn, ones, wout, gw, pw, fn, final):
    R = h.shape[0]
    tm = min(TM_OUT, R)
    row = lambda w: pl.BlockSpec((tm, w), lambda i: (i, 0))
    full = lambda a: pl.BlockSpec(a.shape, lambda i: (0,) * a.ndim, pipeline_mode=pl.Buffered(1))
    return pl.pallas_call(
        functools.partial(_outproj_kernel, final=final),
        grid=(R // tm,),
        in_specs=[row(D_MODEL), row(HG_W), row(HG_W), row(HG_W), row(LRU_W), row(DA_W),
                  pl.BlockSpec((None, tm, PLE_DIM), lambda i: (l, i, 0)),
                  full(hgn), full(ones), _layer_block(wout, l), _layer_block(gw, l), _layer_block(pw, l), full(fn)],
        out_specs=row(D_MODEL),
        out_shape=jax.ShapeDtypeStruct((R, D_MODEL), F32),
        compiler_params=_cparams(("parallel",)),
        name="outproj",
    )(h, of, ob, hg, olru, oda, p, hgn, ones, wout, gw, pw, fn)


def _rope_tables(S):
    half = ROPE_DIM // 2
    inv = ROPE_THETA ** (-jnp.arange(half, dtype=F32) * 2.0 / ROPE_DIM)
    ang = jnp.arange(S, dtype=F32)[:, None] * inv[None, :]
    cos, sin = jnp.cos(ang), jnp.sin(ang)
    ones = jnp.ones((S, DA_QK - ROPE_DIM), F32)
    zeros = jnp.zeros((S, DA_QK - ROPE_DIM), F32)
    z4 = jnp.zeros((S, half), F32)
    rep = lambda t: jnp.tile(t, (1, LANES // DA_QK))
    return (rep(jnp.concatenate([cos, cos, ones], axis=1)),
            rep(jnp.concatenate([-sin, z4, zeros], axis=1)),
            rep(jnp.concatenate([z4, sin, zeros], axis=1)))


def _lru_gate_weights(wa, wx):
    per_half = LRU_HALF // LRU_BLOCK
    halves = []
    for c in range(LRU_W // LRU_HALF):
        tiles = []
        for d in range(2):
            for w in (wa, wx):
                blocks = [w[d, per_half * c + m] for m in range(per_half)]
                tiles.append(jax.scipy.linalg.block_diag(*blocks))
        halves.append(jnp.concatenate(tiles, axis=1))
    return jnp.stack(halves).astype(BF16)


def _trunk(x, p, B, S, prm):
    R = B * S
    h = x.reshape(R, D_MODEL)
    depth = prm["w_in"].shape[0]
    cos_t, s1_t, s2_t = _rope_tables(S)
    lb_all = jnp.cumsum(jax.nn.softmax(prm["hg_lb"].astype(F32), axis=0), axis=0)
    lb_all = lb_all - lb_all[:1]
    lane = np.arange(HG_W)
    ones_bd = jnp.asarray((lane[:, None] // HG_DK) == (lane[None, :] // HG_DK), BF16)
    p3 = p.reshape(depth, R, PLE_DIM)
    for l in range(depth):
        (hq, hkf, hkb, hv, hg, lff, lfb, lx, lg, dqT, dk, dvT, dg) = _inproj(
            h, B, S, l, prm["norm_g"][l][None, :], prm["w_in_bf"], lb_all[l], cos_t, s1_t, s2_t)

        of, ob = _hgrn(hq, lff, hkf, hv, lfb, hkb, B, S)

        cneg = -LRU_C * LOG2E * jax.nn.softplus(-prm["lru_lam"][l].astype(F32))
        bias = jnp.stack([prm["lru_ba"][l, 0], prm["lru_bx"][l, 0], prm["lru_ba"][l, 1], prm["lru_bx"][l, 1]])
        olru = _lru(lx, lg, B, S, prm["lru_conv_w"][l], prm["lru_conv_b"][l][None, :],
                    _lru_gate_weights(prm["lru_wa"][l], prm["lru_wx"][l]), bias, cneg)

        lam_init = 0.8 - 0.6 * math.exp(-0.3 * l)
        lam = (jnp.exp(jnp.sum(prm["da_lq1"][l].astype(F32) * prm["da_lk1"][l].astype(F32)))
               - jnp.exp(jnp.sum(prm["da_lq2"][l].astype(F32) * prm["da_lk2"][l].astype(F32))) + lam_init)
        oda = _attn(lam.reshape(1), dqT, dk, dvT, dg, jnp.tile(prm["da_norm"][l], HEADS)[None, :], B, S, lam_init)

        h = _outproj(h, of, ob, hg, olru, oda, p3, l, jnp.tile(prm["hg_norm"][l], HEADS)[None, :],
                     ones_bd, prm["w_out_bf"], prm["ple_gate_w_bf"], prm["ple_w_bf"], prm["final_norm"][None, :],
                     final=(l == depth - 1))
    return h.reshape(B, S, D_MODEL)


def kernel(x_prompt, x_sample, p_prompt, p_sample, norm_g, w_in, w_out, hg_lb, hg_norm, lru_conv_w, lru_conv_b,
           lru_wa, lru_ba, lru_wx, lru_bx, lru_lam, da_lq1, da_lk1, da_lq2, da_lk2, da_norm, ple_w, ple_gate_w,
           final_norm):
    prm = dict(norm_g=norm_g, w_in=w_in, w_out=w_out, hg_lb=hg_lb, hg_norm=hg_norm, lru_conv_w=lru_conv_w,
               lru_conv_b=lru_conv_b, lru_wa=lru_wa, lru_ba=lru_ba, lru_wx=lru_wx, lru_bx=lru_bx, lru_lam=lru_lam,
               da_lq1=da_lq1, da_lk1=da_lk1, da_lq2=da_lq2, da_lk2=da_lk2, da_norm=da_norm, ple_w=ple_w,
               ple_gate_w=ple_gate_w, final_norm=final_norm)
    prm.update(w_in_bf=w_in.astype(BF16), w_out_bf=w_out.astype(BF16), ple_gate_w_bf=ple_gate_w.astype(BF16),
               ple_w_bf=ple_w.astype(BF16))
    outs = []
    for x, p in ((x_prompt, p_prompt), (x_sample, p_sample)):
        B, S, _ = x.shape
        outs.append(_trunk(x, p, B, S, prm))
    return tuple(outs)
```

```python
import functools
import math

import numpy as np
import jax
import jax.numpy as jnp
from jax import lax
from jax.experimental import pallas as pl
from jax.experimental.pallas import tpu as pltpu

F32 = jnp.float32
BF16 = jnp.bfloat16

D_MODEL = 1024
PLE_DIM = 256
HEADS = 4
HG_W = 256
HG_DK = HG_W // HEADS
LRU_W = 512
LRU_BLOCK = 64
LRU_C = 8.0
DA_W = 256
DA_QK = 32
DA_V = 64
ROPE_THETA = 500000.0
ROPE_DIM = 8
RMS_EPS = 1e-6
LOG2E = 1.4426950408889634
F32_TINY = float(np.finfo(np.float32).tiny)
D_HG_IN = 5 * HG_W
D_LRU_IN = 2 * LRU_W
D_DA_IN = 4 * DA_W

LANES = 128
VMEM_LIMIT = 56 * 1024 * 1024

TM_IN = 1024
TM_OUT = 1024
T_HG = 256
HG_CHUNK = 64
HG_BLK = 16
T_LRU = 256
TQ = 256
TK = 512


def _sigmoid(x):
    return 1.0 / (1.0 + jnp.exp2(x * -LOG2E))


def _cparams(sem):
    return pltpu.CompilerParams(dimension_semantics=sem, vmem_limit_bytes=VMEM_LIMIT)


def _rope(t, cos, s1, s2):
    halves = []
    for j in range(2):
        th = t[:, LANES * j:LANES * (j + 1)]
        up = pltpu.roll(th, LANES - ROPE_DIM // 2, 1)
        dn = pltpu.roll(th, ROPE_DIM // 2, 1)
        halves.append(th * cos + up * s1 + dn * s2)
    return jnp.concatenate(halves, axis=1)


def _inproj_kernel(x_ref, g_ref, w_ref, lb_ref, cos_ref, s1_ref, s2_ref,
                   hq_ref, hkf_ref, hkb_ref, hv_ref, hg_ref, lff_ref, lfb_ref, lx_ref, lg_ref,
                   dqT_ref, dk_ref, dvT_ref, dg_ref):
    x = x_ref[...]
    ms = jnp.mean(x * x, axis=-1, keepdims=True)
    hn = (x * lax.rsqrt(ms + RMS_EPS) * g_ref[...]).astype(BF16)

    z = jnp.dot(hn, w_ref[:, D_HG_IN + D_LRU_IN:], preferred_element_type=F32)
    cos, s1, s2 = cos_ref[...], s1_ref[...], s2_ref[...]
    q = _rope(z[:, 0:DA_W], cos, s1, s2) * (DA_QK ** -0.5 * LOG2E)
    k = _rope(z[:, DA_W:2 * DA_W], cos, s1, s2).astype(BF16)
    dqT_ref[...] = q.T.astype(BF16)
    for hc in range(2 * HEADS):
        dk_ref[hc] = k[:, DA_QK * hc:DA_QK * (hc + 1)]
    dvT_ref[...] = z[:, 2 * DA_W:3 * DA_W].T.astype(BF16)
    dg_ref[...] = z[:, 3 * DA_W:4 * DA_W].astype(BF16)

    z = jnp.dot(hn, w_ref[:, 0:D_HG_IN], preferred_element_type=F32)
    zq = z[:, 0:HG_W]
    hq_ref[...] = (zq * _sigmoid(zq)).astype(BF16)
    for d, lf_ref, k_ref in ((0, lff_ref, hkf_ref), (1, lfb_ref, hkb_ref)):
        zf = z[:, HG_W * (1 + d):HG_W * (2 + d)]
        lb = lb_ref[d:d + 1, :]
        s = _sigmoid(zf)
        lf_ref[...] = jnp.log2(lb + (1.0 - lb) * s)
        k_ref[...] = ((1.0 - lb) * (1.0 - s)).astype(BF16)
    hv_ref[...] = z[:, 3 * HG_W:4 * HG_W].astype(BF16)
    hg_ref[...] = z[:, 4 * HG_W:5 * HG_W].astype(BF16)

    z = jnp.dot(hn, w_ref[:, D_HG_IN:D_HG_IN + D_LRU_IN], preferred_element_type=F32)
    lx_ref[...] = z[:, :LRU_W].astype(BF16)
    lg_ref[...] = z[:, LRU_W:].astype(BF16)


def _layer_block(a, l):
    return pl.BlockSpec((None,) + a.shape[1:], lambda i: (l,) + (0,) * (a.ndim - 1), pipeline_mode=pl.Buffered(1))


def _inproj(x2d, B, S, l, g, w_in, lb, cos_t, s1_t, s2_t):
    R = B * S
    tm = min(TM_IN, S)
    nt = S // tm
    row = lambda w: pl.BlockSpec((tm, w), lambda i: (i, 0))
    full = lambda a: pl.BlockSpec(a.shape, lambda i: (0,) * a.ndim, pipeline_mode=pl.Buffered(1))
    tab = pl.BlockSpec((tm, LANES), lambda i: (i % nt, 0))
    tr = pl.BlockSpec((None, DA_W, tm), lambda i: (i // nt, 0, i % nt))
    sds = jax.ShapeDtypeStruct
    out_shape = (
        sds((R, HG_W), BF16), sds((R, HG_W), BF16), sds((R, HG_W), BF16), sds((R, HG_W), BF16), sds((R, HG_W), BF16),
        sds((R, HG_W), F32), sds((R, HG_W), F32),
        sds((R, LRU_W), BF16), sds((R, LRU_W), BF16),
        sds((B, DA_W, S), BF16), sds((B, 2 * HEADS, S, DA_QK), BF16), sds((B, DA_W, S), BF16), sds((R, DA_W), BF16),
    )
    khead = pl.BlockSpec((None, 2 * HEADS, tm, DA_QK), lambda i: (i // nt, 0, i % nt, 0))
    out_specs = (
        row(HG_W), row(HG_W), row(HG_W), row(HG_W), row(HG_W), row(HG_W), row(HG_W),
        row(LRU_W), row(LRU_W),
        tr, khead, tr, row(DA_W),
    )
    return pl.pallas_call(
        _inproj_kernel,
        grid=(R // tm,),
        in_specs=[row(D_MODEL), full(g), _layer_block(w_in, l), full(lb), tab, tab, tab],
        out_specs=out_specs,
        out_shape=out_shape,
        compiler_params=_cparams(("parallel",)),
        name="inproj",
    )(x2d, g, w_in, lb, cos_t, s1_t, s2_t)


def _split2(x):
    hi = x.astype(BF16)
    lo = (x - hi.astype(F32)).astype(BF16)
    return hi, lo


def _dot_t(a, b):
    return lax.dot_general(a, b, (((1,), (1,)), ((), ())), preferred_element_type=F32)


def _dot_a(a, b):
    return lax.dot_general(a, b, (((0,), (0,)), ((), ())), preferred_element_type=F32)


def _head_rows(x_bf, hm_ref):
    zeros = jnp.zeros((HG_BLK, LANES), BF16)
    rows = []
    for h in range(HEADS):
        half = x_bf[:, LANES * (h // 2):LANES * (h // 2 + 1)] * hm_ref[h % 2]
        rows.append(jnp.concatenate([half, zeros] if h < 2 else [zeros, half], axis=1))
    return jnp.concatenate(rows, axis=0)


def _hgrn_tile(q_ref, lf_ref, k_ref, v_ref, ltri_ref, esel_ref, gmask_ref, hm_ref, bd_ref, s_ref, o_ref, reverse):
    T = T_HG
    nblk = HG_CHUNK // HG_BLK
    q = q_ref[...].astype(F32)
    k = k_ref[...].astype(F32)
    v_bf = v_ref[...]
    lf = lf_ref[...]

    ltri = ltri_ref[...]
    c = sum(jnp.dot(ltri, piece, preferred_element_type=F32) for piece in _split2(lf))

    nb = T // HG_BLK
    half = HG_BLK // 2
    c4 = c.reshape(nb, 2, half, HG_W)
    ck4 = (c - jnp.log2(k)).reshape(nb, 2, half, HG_W)
    q4 = q.reshape(nb, 2, half, HG_W)
    far = 0 if reverse else 1
    dacc = jnp.zeros((T, HG_W), F32)
    dfar = jnp.zeros((T // 2, HG_W), F32)
    for jl in range(HG_BLK):
        cj = ck4[:, jl // half, jl % half:jl % half + 1, :][:, None]
        if jl // half == far:
            xj = q4[:, far:far + 1] * jnp.exp2(jnp.minimum(c4[:, far:far + 1] - cj, 0.0))
            dfar = dfar + jnp.dot(xj.reshape(T // 2, HG_W).astype(BF16), esel_ref[jl], preferred_element_type=F32)
        else:
            near = 1 - far
            x_near = q4[:, near:near + 1] * jnp.exp2(jnp.minimum(c4[:, near:near + 1] - cj, 0.0))
            x_far = q4[:, far:far + 1] * jnp.exp2(c4[:, far:far + 1] - cj)
            xj = jnp.concatenate([x_near, x_far] if far else [x_far, x_near], axis=1)
            dacc = dacc + jnp.dot(xj.reshape(T, HG_W).astype(BF16), esel_ref[jl], preferred_element_type=F32)
    d4 = dacc.reshape(nb, 2, half, HG_W)
    dfar4 = d4[:, far:far + 1] + dfar.reshape(nb, 1, half, HG_W)
    d4 = jnp.concatenate([d4[:, 0:1], dfar4] if far else [dfar4, d4[:, 1:2]], axis=1)
    dacc = d4.reshape(T, HG_W) * gmask_ref[...]
    yield

    bd = bd_ref[...]
    zpad64 = jnp.zeros((HG_CHUNK, HG_W), BF16)
    zcol = jnp.zeros((HG_BLK, LANES), F32)
    chunks = list(range(T // HG_CHUNK))
    per_chunk = {}
    for n in chunks:
        r0 = n * HG_CHUNK
        cn = c[r0:r0 + HG_CHUNK]
        qn = q[r0:r0 + HG_CHUNK]
        kn = k[r0:r0 + HG_CHUNK]
        vn = v_bf[r0:r0 + HG_CHUNK]
        ctot = cn[0:1] if reverse else cn[HG_CHUNK - 1:HG_CHUNK]
        qt = (qn * jnp.exp2(cn)).astype(BF16)
        kt = (kn * jnp.exp2(ctot - cn)).astype(BF16)
        upd = bd * _dot_a(vn, kt)
        vbd = jnp.concatenate([_head_rows(vn[HG_BLK * b:HG_BLK * (b + 1)], hm_ref) for b in range(nblk)], axis=0)
        a_rows = []
        for i in range(nblk):
            rows = slice(HG_BLK * i, HG_BLK * (i + 1))
            d_i = dacc[r0 + HG_BLK * i:r0 + HG_BLK * (i + 1)]
            others = list(range(i + 1, nblk)) if reverse else list(range(i))
            if not others:
                a_rows.append(d_i)
                continue
            ref_row = cn[HG_BLK * (i + 1):HG_BLK * (i + 1) + 1] if reverse else cn[HG_BLK * i - 1:HG_BLK * i]
            qh = (qn[rows] * jnp.exp2(cn[rows] - ref_row)).astype(BF16)
            pieces = []
            for b in others:
                brow = slice(HG_BLK * b, HG_BLK * (b + 1))
                kh = (kn[brow] * jnp.exp2(ref_row - cn[brow])).astype(BF16)
                pieces.append(_head_rows(kh, hm_ref))
            if len(pieces) % 2:
                pieces = ([zpad64] + pieces) if reverse else (pieces + [zpad64])
            r_i = _dot_t(qh, jnp.concatenate(pieces, axis=0))
            if r_i.shape[1] == LANES:
                r_i = jnp.concatenate([zcol, r_i] if reverse else [r_i, zcol], axis=1)
            a_rows.append(r_i + d_i)
        a_t = jnp.concatenate(a_rows, axis=0).astype(BF16)
        per_chunk[n] = (qt, jnp.exp2(ctot), upd, a_t, vbd)
    yield

    o_intra = {n: jnp.dot(per_chunk[n][3], per_chunk[n][4], preferred_element_type=F32) for n in chunks}
    s_t = s_ref[...]
    for n in (reversed(chunks) if reverse else chunks):
        qt, decay, upd, _, _ = per_chunk[n]
        r0 = n * HG_CHUNK
        o_ref[r0:r0 + HG_CHUNK, :] = _dot_t(qt, s_t.astype(BF16)) + o_intra[n]
        s_t = s_t * decay + upd
    s_ref[...] = s_t
    yield


def _hgrn_kernel(qf_ref, lff_ref, kf_ref, vf_ref, qb_ref, lfb_ref, kb_ref, vb_ref,
                 ltf_ref, ltb_ref, esel_ref, gmf_ref, gmb_ref, hm_ref, bd_ref,
                 of_ref, ob_ref, sf_ref, sb_ref):
    @pl.when(pl.program_id(1) == 0)
    def _():
        sf_ref[...] = jnp.zeros_like(sf_ref)
        sb_ref[...] = jnp.zeros_like(sb_ref)

    tiles = [
        _hgrn_tile(qf_ref, lff_ref, kf_ref, vf_ref, ltf_ref, esel_ref, gmf_ref, hm_ref, bd_ref, sf_ref, of_ref, False),
        _hgrn_tile(qb_ref, lfb_ref, kb_ref, vb_ref, ltb_ref, esel_ref, gmb_ref, hm_ref, bd_ref, sb_ref, ob_ref, True),
    ]
    for _ in range(3):
        for t in tiles:
            next(t)


@functools.lru_cache(maxsize=None)
def _hgrn_consts():
    T = T_HG
    t = np.arange(T)
    same_chunk = (t[:, None] // HG_CHUNK) == (t[None, :] // HG_CHUNK)
    ltf = (same_chunk & (t[None, :] <= t[:, None])).astype(np.float32)
    ltb = (same_chunk & (t[None, :] >= t[:, None])).astype(np.float32)
    lane = np.arange(HG_W)
    col = np.arange(HG_W)
    esel = np.zeros((HG_BLK, HG_W, HG_W), np.float32)
    grp = HEADS * HG_BLK
    for jl in range(HG_BLK):
        esel[jl] = ((lane[:, None] // HG_DK) == ((col[None, :] % grp) // HG_BLK)) & ((col[None, :] % HG_BLK) == jl)
    blk_of_row = (t % HG_CHUNK) // HG_BLK
    own = (col[None, :] // grp) == blk_of_row[:, None]
    jl_col = col[None, :] % HG_BLK
    gmf = (own & (jl_col <= (t % HG_BLK)[:, None])).astype(np.float32)
    gmb = (own & (jl_col >= (t % HG_BLK)[:, None])).astype(np.float32)
    hm = np.stack([(np.arange(LANES) < HG_DK), (np.arange(LANES) >= HG_DK)]).astype(np.float32)
    hm = np.broadcast_to(hm[:, None, :], (2, HG_BLK, LANES)).copy()
    bd = ((lane[:, None] // HG_DK) == (lane[None, :] // HG_DK)).astype(np.float32)
    return ltf, ltb, esel, gmf, gmb, hm, bd


def _hgrn(hq, lff, hkf, hv, lfb, hkb, B, S):
    T = T_HG
    nt = S // T
    ltf, ltb, esel, gmf, gmb, hm, bd = _hgrn_consts()
    consts = (jnp.asarray(ltf, BF16), jnp.asarray(ltb, BF16), jnp.asarray(esel, BF16),
              jnp.asarray(gmf, F32), jnp.asarray(gmb, F32), jnp.asarray(hm, BF16), jnp.asarray(bd, F32))
    r3 = lambda a: a.reshape(B, S, HG_W)
    fwd = pl.BlockSpec((None, T, HG_W), lambda b, i: (b, i, 0))
    bwd = pl.BlockSpec((None, T, HG_W), lambda b, i: (b, nt - 1 - i, 0))
    full = lambda a: pl.BlockSpec(a.shape, lambda b, i: (0,) * a.ndim, pipeline_mode=pl.Buffered(1))
    of, ob = pl.pallas_call(
        _hgrn_kernel,
        grid=(B, nt),
        in_specs=[fwd, fwd, fwd, fwd, bwd, bwd, bwd, bwd] + [full(a) for a in consts],
        out_specs=(fwd, bwd),
        out_shape=(jax.ShapeDtypeStruct((B, S, HG_W), F32), jax.ShapeDtypeStruct((B, S, HG_W), F32)),
        scratch_shapes=[pltpu.VMEM((HG_W, HG_W), F32), pltpu.VMEM((HG_W, HG_W), F32)],
        compiler_params=_cparams(("parallel", "arbitrary")),
        name="hgrn",
    )(r3(hq), r3(lff), r3(hkf), r3(hv), r3(hq), r3(lfb), r3(hkb), r3(hv), *consts)
    return of.reshape(B * S, HG_W), ob.reshape(B * S, HG_W)


LRU_HALF = LRU_W // 2
LRU_HALO = 16


SUBLANES = 8
LRU_SEG = 16
LRU_NSEG = T_LRU // LRU_SEG
LRU_PITCH = LRU_SEG + SUBLANES


def _scan_chunk(a, b, carry, reverse, a_s, b_s, h_s):
    T, C = a.shape
    L, P, G = LRU_SEG, LRU_PITCH, LRU_NSEG // SUBLANES
    outs, carries = [], []
    for slab in range(C // LANES):
        lanes = slice(LANES * slab, LANES * (slab + 1))
        for s in range(LRU_NSEG):
            a_s[slab, P * s:P * s + L, :] = a[L * s:L * (s + 1), lanes]
            b_s[slab, P * s:P * s + L, :] = b[L * s:L * (s + 1), lanes]
        h = [jnp.zeros((SUBLANES, LANES), F32)] * G
        p = [jnp.ones((SUBLANES, LANES), F32)] * G
        h_loc = [[None] * L for _ in range(G)]
        p_loc = [[None] * L for _ in range(G)]
        for v in (range(L - 1, -1, -1) if reverse else range(L)):
            for g in range(G):
                rows = pl.ds(P * SUBLANES * g + v, SUBLANES, stride=P)
                av = a_s[slab, rows, :]
                h[g] = av * h[g] + b_s[slab, rows, :]
                p[g] = av * p[g]
                h_loc[g][v], p_loc[g][v] = h[g], p[g]
        c = carry[:, lanes]
        inflow = [None] * LRU_NSEG
        for s in (range(LRU_NSEG - 1, -1, -1) if reverse else range(LRU_NSEG)):
            g, r = divmod(s, SUBLANES)
            inflow[s] = c
            c = p[g][r:r + 1] * c + h[g][r:r + 1]
        for g in range(G):
            cin = jnp.concatenate(inflow[SUBLANES * g:SUBLANES * (g + 1)], axis=0)
            for v in range(L):
                h_s[slab, pl.ds(P * SUBLANES * g + v, SUBLANES, stride=P), :] = h_loc[g][v] + p_loc[g][v] * cin
        outs.append(jnp.concatenate([h_s[slab, P * s:P * s + L, :] for s in range(LRU_NSEG)], axis=0))
        carries.append(c)
    return jnp.concatenate(outs, axis=1), jnp.concatenate(carries, axis=1)


def _lru_kernel(lx_ref, lg_ref, cw_ref, cb_ref, w_ref, bias_ref, cneg_ref, out_ref, u_ref, ub_ref, hf_ref, hb_ref,
                *stage_refs, S):
    T = T_LRU
    n = S // T
    C = LRU_HALF
    cw = cw_ref[...]
    cb = cb_ref[...]

    def conv_body(i, _):
        t0 = pl.multiple_of(i * T, T)
        cur = lx_ref[pl.ds(t0, T), :].astype(F32)
        ps = pl.multiple_of(jnp.maximum(t0 - LRU_HALO, 0), LRU_HALO)
        prev = jnp.where(i > 0, lx_ref[pl.ds(ps, LRU_HALO), :].astype(F32), 0.0)
        ns = pl.multiple_of(jnp.minimum(t0 + T, S - LRU_HALO), LRU_HALO)
        nxt = jnp.where(i < n - 1, lx_ref[pl.ds(ns, LRU_HALO), :].astype(F32), 0.0)
        ext = jnp.concatenate([prev, cur, nxt], axis=0)
        h = LRU_HALO
        u = (cb + ext[h - 2:h - 2 + T] * cw[0:1] + ext[h - 1:h - 1 + T] * cw[1:2]
             + cur * cw[2:3] + ext[h + 1:h + 1 + T] * cw[3:4])
        u_ref[pl.ds(t0, T), :] = u
        ub_ref[pl.ds(t0, T), :] = u.astype(BF16)
        return 0

    lax.fori_loop(0, n, conv_body, 0)

    def gates(d, i):
        rows = pl.ds(pl.multiple_of(i * T, T), T)
        u = u_ref[rows, :]
        pre = jnp.dot(ub_ref[rows, :], w_ref[:, 2 * C * d:2 * C * (d + 1)], preferred_element_type=F32)
        r = _sigmoid(pre[:, :C] + bias_ref[2 * d:2 * d + 1, :])
        ig = _sigmoid(pre[:, C:] + bias_ref[2 * d + 1:2 * d + 2, :])
        a = jnp.exp2(cneg_ref[d:d + 1, :] * r)
        y = 1.0 - a * a
        b = y * lax.rsqrt(jnp.maximum(y, F32_TINY)) * (ig * u)
        return a, b

    def emit(rows, h):
        g = lg_ref[rows, :].astype(F32)
        out_ref[rows, :] = (h * (g * _sigmoid(g))).astype(BF16)

    def step(i, carry, meet):
        cf, cr = carry
        j = n - 1 - i
        rows_i = pl.ds(pl.multiple_of(i * T, T), T)
        rows_j = pl.ds(pl.multiple_of(j * T, T), T)
        af, bf = gates(0, i)
        ar, br = gates(1, j)
        hf, cf = _scan_chunk(af, bf, cf, False, *stage_refs[0:3])
        hr, cr = _scan_chunk(ar, br, cr, True, *stage_refs[3:6])
        if meet:
            emit(rows_i, hf + hb_ref[rows_i, :])
            emit(rows_j, hf_ref[rows_j, :] + hr)
        else:
            hf_ref[rows_i, :] = hf
            hb_ref[rows_j, :] = hr
        return cf, cr

    zero = jnp.zeros((1, C), F32)
    unroll = 2 if n % 4 == 0 else 1
    carry = lax.fori_loop(0, n // 2, functools.partial(step, meet=False), (zero, zero), unroll=unroll)
    lax.fori_loop(n // 2, n, functools.partial(step, meet=True), carry, unroll=unroll)


def _lru(lx, lg, B, S, cw, cb, w, bias, cneg):
    C = LRU_HALF
    seq = pl.BlockSpec((None, S, C), lambda b, c: (b, 0, c))
    vec = lambda r: pl.BlockSpec((r, C), lambda b, c: (0, c))
    out = pl.pallas_call(
        functools.partial(_lru_kernel, S=S),
        grid=(B, LRU_W // C),
        in_specs=[seq, seq, vec(4), vec(1), pl.BlockSpec((None, C, 4 * C), lambda b, c: (c, 0, 0)), vec(4), vec(2)],
        out_specs=seq,
        out_shape=jax.ShapeDtypeStruct((B, S, LRU_W), BF16),
        scratch_shapes=[pltpu.VMEM((S, C), F32), pltpu.VMEM((S, C), BF16), pltpu.VMEM((S, C), F32), pltpu.VMEM((S, C), F32)]
        + [pltpu.VMEM((C // LANES, LRU_NSEG * LRU_PITCH, LANES), F32)] * 6,
        compiler_params=_cparams(("parallel", "parallel")),
        name="lru",
    )(lx.reshape(B, S, LRU_W), lg.reshape(B, S, LRU_W), cw, cb, w, bias, cneg)
    return out.reshape(B * S, LRU_W)


PV_ROWS = DA_V + 16
M_INIT = -1e30
ATTN_GROUP = 2


def _attn_kernel(lam_ref, qT_ref, qTn_ref, k_ref, vT_ref, dg_ref, ng_ref, out_ref, rhs_ref, m_ref, acc_ref, sa_ref, sb_ref,
                 *, S, tk, lam_init):
    nk = S // tk
    qi = pl.program_id(1)
    cur_q = qi % 2
    nxt_q = 1 - cur_q
    ones = jnp.ones((PV_ROWS - DA_V, tk), BF16)

    def make_rhs(q_ref, slot):
        for hc in range(2 * HEADS):
            rhs_ref[slot, hc] = q_ref[DA_QK * hc:DA_QK * (hc + 1), :]

    def stage(j_next, q_slot, next_ref, j_cur, cur_ref):
        krows = pl.ds(pl.multiple_of(j_next * tk, tk), tk)
        if j_cur is not None:
            vT = vT_ref[:, pl.ds(pl.multiple_of(j_cur * tk, tk), tk)]
        for hc in range(2 * HEADS):
            if hc % ATTN_GROUP == 0:
                for g in range(hc, hc + ATTN_GROUP):
                    next_ref[g] = jnp.dot(k_ref[g, krows, :], rhs_ref[q_slot, g], preferred_element_type=F32)
            if j_cur is not None:
                h = hc // 2
                s = cur_ref[hc]
                m_old = m_ref[hc]
                m_new = jnp.maximum(m_old, jnp.max(s, axis=0, keepdims=True))
                p = jnp.exp2(s - m_new).astype(BF16)
                lhs = jnp.concatenate([vT[DA_V * h:DA_V * (h + 1), :], ones], axis=0)
                acc_ref[hc] = jnp.exp2(m_old - m_new) * acc_ref[hc] + jnp.dot(lhs, p, preferred_element_type=F32)
                m_ref[hc] = m_new

    @pl.when(qi == 0)
    def _():
        make_rhs(qT_ref, 0)
        stage(0, 0, sa_ref, None, None)

    make_rhs(qTn_ref, nxt_q)
    m_ref[...] = jnp.full(m_ref.shape, M_INIT, F32)
    acc_ref[...] = jnp.zeros_like(acc_ref)

    def body(i, _):
        stage(2 * i + 1, cur_q, sb_ref, 2 * i, sa_ref)
        stage(2 * i + 2, cur_q, sa_ref, 2 * i + 1, sb_ref)
        return 0

    lax.fori_loop(0, nk // 2 - 1, body, 0)
    stage(nk - 1, cur_q, sb_ref, nk - 2, sa_ref)
    stage(0, nxt_q, sa_ref, nk - 1, sb_ref)

    lam = lam_ref[0]
    outs = []
    for h in range(HEADS):
        a0 = acc_ref[2 * h]
        a1 = acc_ref[2 * h + 1]
        o = a0[:DA_V] * (1.0 / a0[DA_V:DA_V + 1]) - lam * (a1[:DA_V] * (1.0 / a1[DA_V:DA_V + 1]))
        ms = jnp.mean(o * o, axis=0, keepdims=True)
        outs.append(o * lax.rsqrt(ms + RMS_EPS))
    o = jnp.concatenate(outs, axis=0).T
    g = dg_ref[...].astype(F32)
    out_ref[...] = (o * ng_ref[...] * (1.0 - lam_init) * (g * _sigmoid(g))).astype(BF16)


def _attn(lam, dqT, dk, dvT, dg, ng, B, S, lam_init):
    nq = S // TQ
    tk = min(TK, S // 2)
    out = pl.pallas_call(
        functools.partial(_attn_kernel, S=S, tk=tk, lam_init=lam_init),
        grid=(B, nq),
        in_specs=[
            pl.BlockSpec(memory_space=pltpu.SMEM),
            pl.BlockSpec((None, DA_W, TQ), lambda b, i: (b, 0, i)),
            pl.BlockSpec((None, DA_W, TQ), lambda b, i: (b, 0, jnp.minimum(i + 1, nq - 1))),
            pl.BlockSpec((None, 2 * HEADS, S, DA_QK), lambda b, i: (b, 0, 0, 0)),
            pl.BlockSpec((None, DA_W, S), lambda b, i: (b, 0, 0)),
            pl.BlockSpec((None, TQ, DA_W), lambda b, i: (b, i, 0)),
            pl.BlockSpec((1, DA_W), lambda b, i: (0, 0)),
        ],
        out_specs=pl.BlockSpec((None, TQ, DA_W), lambda b, i: (b, i, 0)),
        out_shape=jax.ShapeDtypeStruct((B, S, DA_W), BF16),
        scratch_shapes=[
            pltpu.VMEM((2, 2 * HEADS, DA_QK, TQ), BF16),
            pltpu.VMEM((2 * HEADS, 1, TQ), F32),
            pltpu.VMEM((2 * HEADS, PV_ROWS, TQ), F32),
            pltpu.VMEM((2 * HEADS, tk, TQ), F32),
            pltpu.VMEM((2 * HEADS, tk, TQ), F32),
        ],
        compiler_params=_cparams(("parallel", "arbitrary")),
        name="attn",
    )(lam, dqT, dqT, dk, dvT, dg.reshape(B, S, DA_W), ng)
    return out.reshape(B * S, DA_W)


def _outproj_kernel(h_ref, of_ref, ob_ref, hg_ref, olru_ref, oda_ref, p_ref, hgn_ref, ones_ref,
                    wout_ref, gw_ref, pw_ref, fn_ref, out_ref, *, final):
    acc = h_ref[...]
    acc = acc + jnp.dot(olru_ref[...], wout_ref[HG_W:HG_W + LRU_W, :], preferred_element_type=F32)
    acc = acc + jnp.dot(oda_ref[...], wout_ref[HG_W + LRU_W:, :], preferred_element_type=F32)
    ple = jnp.dot(p_ref[...].astype(BF16), pw_ref[...], preferred_element_type=F32)

    o = of_ref[...] + ob_ref[...]
    sq = o * o
    hi = sq.astype(BF16)
    lo = (sq - hi.astype(F32)).astype(BF16)
    ones = ones_ref[...]
    ssum = jnp.dot(hi, ones, preferred_element_type=F32) + jnp.dot(lo, ones, preferred_element_type=F32)
    g = hg_ref[...].astype(F32)
    ohg = (o * lax.rsqrt(ssum * (1.0 / HG_DK) + RMS_EPS) * hgn_ref[...] * (g * _sigmoid(g))).astype(BF16)

    acc = acc + jnp.dot(ohg, wout_ref[0:HG_W, :], preferred_element_type=F32)
    gate = _sigmoid(jnp.dot(acc.astype(BF16), gw_ref[...], preferred_element_type=F32))
    h2 = acc + gate * ple
    if final:
        ms = jnp.mean(h2 * h2, axis=-1, keepdims=True)
        h2 = h2 * lax.rsqrt(ms + RMS_EPS) * fn_ref[...]
    out_ref[...] = h2


def _outproj(h, of, ob, hg, olru, oda, p, l, hgn, ones, wout, gw, pw, fn, final):
    R = h.shape[0]
    tm = min(TM_OUT, R)
    row = lambda w: pl.BlockSpec((tm, w), lambda i: (i, 0))
    full = lambda a: pl.BlockSpec(a.shape, lambda i: (0,) * a.ndim, pipeline_mode=pl.Buffered(1))
    return pl.pallas_call(
        functools.partial(_outproj_kernel, final=final),
        grid=(R // tm,),
        in_specs=[row(D_MODEL), row(HG_W), row(HG_W), row(HG_W), row(LRU_W), row(DA_W),
                  pl.BlockSpec((None, tm, PLE_DIM), lambda i: (l, i, 0)),
                  full(hgn), full(ones), _layer_block(wout, l), _layer_block(gw, l), _layer_block(pw, l), full(fn)],
        out_specs=row(D_MODEL),
        out_shape=jax.ShapeDtypeStruct((R, D_MODEL), F32),
        compiler_params=_cparams(("parallel",)),
        name="outproj",
    )(h, of, ob, hg, olru, oda, p, hgn, ones, wout, gw, pw, fn)


def _rope_tables(S):
    half = ROPE_DIM // 2
    inv = ROPE_THETA ** (-jnp.arange(half, dtype=F32) * 2.0 / ROPE_DIM)
    ang = jnp.arange(S, dtype=F32)[:, None] * inv[None, :]
    cos, sin = jnp.cos(ang), jnp.sin(ang)
    ones = jnp.ones((S, DA_QK - ROPE_DIM), F32)
    zeros = jnp.zeros((S, DA_QK - ROPE_DIM), F32)
    z4 = jnp.zeros((S, half), F32)
    rep = lambda t: jnp.tile(t, (1, LANES // DA_QK))
    return (rep(jnp.concatenate([cos, cos, ones], axis=1)),
            rep(jnp.concatenate([-sin, z4, zeros], axis=1)),
            rep(jnp.concatenate([z4, sin, zeros], axis=1)))


def _lru_gate_weights(wa, wx):
    per_half = LRU_HALF // LRU_BLOCK
    halves = []
    for c in range(LRU_W // LRU_HALF):
        tiles = []
        for d in range(2):
            for w in (wa, wx):
                blocks = [w[d, per_half * c + m] for m in range(per_half)]
                tiles.append(jax.scipy.linalg.block_diag(*blocks))
        halves.append(jnp.concatenate(tiles, axis=1))
    return jnp.stack(halves).astype(BF16)


def _trunk(x, p, B, S, prm):
    R = B * S
    h = x.reshape(R, D_MODEL)
    depth = prm["w_in"].shape[0]
    cos_t, s1_t, s2_t = _rope_tables(S)
    lb_all = jnp.cumsum(jax.nn.softmax(prm["hg_lb"].astype(F32), axis=0), axis=0)
    lb_all = lb_all - lb_all[:1]
    lane = np.arange(HG_W)
    ones_bd = jnp.asarray((lane[:, None] // HG_DK) == (lane[None, :] // HG_DK), BF16)
    p3 = p.reshape(depth, R, PLE_DIM)
    for l in range(depth):
        (hq, hkf, hkb, hv, hg, lff, lfb, lx, lg, dqT, dk, dvT, dg) = _inproj(
            h, B, S, l, prm["norm_g"][l][None, :], prm["w_in_bf"], lb_all[l], cos_t, s1_t, s2_t)

        of, ob = _hgrn(hq, lff, hkf, hv, lfb, hkb, B, S)

        cneg = -LRU_C * LOG2E * jax.nn.softplus(-prm["lru_lam"][l].astype(F32))
        bias = jnp.stack([prm["lru_ba"][l, 0], prm["lru_bx"][l, 0], prm["lru_ba"][l, 1], prm["lru_bx"][l, 1]])
        olru = _lru(lx, lg, B, S, prm["lru_conv_w"][l], prm["lru_conv_b"][l][None, :],
                    _lru_gate_weights(prm["lru_wa"][l], prm["lru_wx"][l]), bias, cneg)

        lam_init = 0.8 - 0.6 * math.exp(-0.3 * l)
        lam = (jnp.exp(jnp.sum(prm["da_lq1"][l].astype(F32) * prm["da_lk1"][l].astype(F32)))
               - jnp.exp(jnp.sum(prm["da_lq2"][l].astype(F32) * prm["da_lk2"][l].astype(F32))) + lam_init)
        oda = _attn(lam.reshape(1), dqT, dk, dvT, dg, jnp.tile(prm["da_norm"][l], HEADS)[None, :], B, S, lam_init)

        h = _outproj(h, of, ob, hg, olru, oda, p3, l, jnp.tile(prm["hg_norm"][l], HEADS)[None, :],
                     ones_bd, prm["w_out_bf"], prm["ple_gate_w_bf"], prm["ple_w_bf"], prm["final_norm"][None, :],
                     final=(l == depth - 1))
    return h.reshape(B, S, D_MODEL)


def kernel(x_prompt, x_sample, p_prompt, p_sample, norm_g, w_in, w_out, hg_lb, hg_norm, lru_conv_w, lru_conv_b,
           lru_wa, lru_ba, lru_wx, lru_bx, lru_lam, da_lq1, da_lk1, da_lq2, da_lk2, da_norm, ple_w, ple_gate_w,
           final_norm):
    prm = dict(norm_g=norm_g, w_in=w_in, w_out=w_out, hg_lb=hg_lb, hg_norm=hg_norm, lru_conv_w=lru_conv_w,
               lru_conv_b=lru_conv_b, lru_wa=lru_wa, lru_ba=lru_ba, lru_wx=lru_wx, lru_bx=lru_bx, lru_lam=lru_lam,
               da_lq1=da_lq1, da_lk1=da_lk1, da_lq2=da_lq2, da_lk2=da_lk2, da_norm=da_norm, ple_w=ple_w,
               ple_gate_w=ple_gate_w, final_norm=final_norm)
    prm.update(w_in_bf=w_in.astype(BF16), w_out_bf=w_out.astype(BF16), ple_gate_w_bf=ple_gate_w.astype(BF16),
               ple_w_bf=ple_w.astype(BF16))
    outs = []
    for x, p in ((x_prompt, p_prompt), (x_sample, p_sample)):
        B, S, _ = x.shape
        outs.append(_trunk(x, p, B, S, prm))
    return tuple(outs)
```

```python
import functools
import math

import numpy as np
import jax
import jax.numpy as jnp
from jax import lax
from jax.experimental import pallas as pl
from jax.experimental.pallas import tpu as pltpu

F32 = jnp.float32
BF16 = jnp.bfloat16
F8 = jnp.float8_e4m3fn

D_MODEL = 1024
PLE_DIM = 256
HEADS = 4
HG_W = 256
HG_DK = HG_W // HEADS
LRU_W = 512
LRU_BLOCK = 64
LRU_C = 8.0
DA_W = 256
DA_QK = 32
DA_V = 64
ROPE_THETA = 500000.0
ROPE_DIM = 8
RMS_EPS = 1e-6
LOG2E = 1.4426950408889634
F32_TINY = float(np.finfo(np.float32).tiny)
D_HG_IN = 5 * HG_W
D_LRU_IN = 2 * LRU_W
D_DA_IN = 4 * DA_W

LANES = 128
VMEM_LIMIT = 56 * 1024 * 1024

TM_IN = 1024
TM_OUT = 1024
T_HG = 256
HG_CHUNK = 64
HG_BLK = 16
T_LRU = 256
TQ = 256
TK = 512


def _sigmoid(x):
    return 1.0 / (1.0 + jnp.exp2(x * -LOG2E))


def _cparams(sem):
    return pltpu.CompilerParams(dimension_semantics=sem, vmem_limit_bytes=VMEM_LIMIT)


def _rope(t, cos, s1, s2):
    halves = []
    for j in range(2):
        th = t[:, LANES * j:LANES * (j + 1)]
        up = pltpu.roll(th, LANES - ROPE_DIM // 2, 1)
        dn = pltpu.roll(th, ROPE_DIM // 2, 1)
        halves.append(th * cos + up * s1 + dn * s2)
    return jnp.concatenate(halves, axis=1)


def _inproj_kernel(x_ref, g_ref, w_ref, lb_ref, cos_ref, s1_ref, s2_ref,
                   hq_ref, hkf_ref, hkb_ref, hv_ref, hg_ref, lff_ref, lfb_ref, lx_ref, lg_ref,
                   dqT_ref, dk_ref, dvT_ref, dg_ref):
    x = x_ref[...]
    ms = jnp.mean(x * x, axis=-1, keepdims=True)
    hn = (x * lax.rsqrt(ms + RMS_EPS) * g_ref[...]).astype(BF16)

    z = jnp.dot(hn, w_ref[:, D_HG_IN + D_LRU_IN:], preferred_element_type=F32)
    cos, s1, s2 = cos_ref[...], s1_ref[...], s2_ref[...]
    q = _rope(z[:, 0:DA_W], cos, s1, s2) * (DA_QK ** -0.5 * LOG2E)
    k = _rope(z[:, DA_W:2 * DA_W], cos, s1, s2)
    dqT_ref[...] = q.T.astype(BF16)
    slot = lax.broadcasted_iota(jnp.int32, (k.shape[0], LANES), 1) // DA_QK
    for half in range(DA_W // LANES):
        kk = k[:, LANES * half:LANES * (half + 1)]
        hi = kk.astype(F8).astype(F32)
        lo = kk - hi
        r1 = pltpu.roll(lo, DA_QK, 1)
        r2 = pltpu.roll(hi, 2 * DA_QK, 1)
        r3 = pltpu.roll(lo, 3 * DA_QK, 1)
        for j in range(LANES // DA_QK):
            s = (slot - j) & (LANES // DA_QK - 1)
            row = jnp.where(s == 0, hi, jnp.where(s == 1, r1, jnp.where(s == 2, r2, r3)))
            dk_ref[(LANES // DA_QK) * half + j] = row.astype(F8)
    dvT_ref[...] = z[:, 2 * DA_W:3 * DA_W].T.astype(BF16)
    dg_ref[...] = z[:, 3 * DA_W:4 * DA_W].astype(BF16)

    z = jnp.dot(hn, w_ref[:, 0:D_HG_IN], preferred_element_type=F32)
    zq = z[:, 0:HG_W]
    hq_ref[...] = (zq * _sigmoid(zq)).astype(BF16)
    for d, lf_ref, k_ref in ((0, lff_ref, hkf_ref), (1, lfb_ref, hkb_ref)):
        zf = z[:, HG_W * (1 + d):HG_W * (2 + d)]
        lb = lb_ref[d:d + 1, :]
        s = _sigmoid(zf)
        lf_ref[...] = jnp.log2(lb + (1.0 - lb) * s)
        k_ref[...] = ((1.0 - lb) * (1.0 - s)).astype(BF16)
    hv_ref[...] = z[:, 3 * HG_W:4 * HG_W].astype(BF16)
    hg_ref[...] = z[:, 4 * HG_W:5 * HG_W].astype(BF16)

    z = jnp.dot(hn, w_ref[:, D_HG_IN:D_HG_IN + D_LRU_IN], preferred_element_type=F32)
    lx_ref[...] = z[:, :LRU_W].astype(BF16)
    lg_ref[...] = z[:, LRU_W:].astype(BF16)


def _layer_block(a, l):
    return pl.BlockSpec((None,) + a.shape[1:], lambda i: (l,) + (0,) * (a.ndim - 1), pipeline_mode=pl.Buffered(1))


def _inproj(x2d, B, S, l, g, w_in, lb, cos_t, s1_t, s2_t):
    R = B * S
    tm = min(TM_IN, S)
    nt = S // tm
    row = lambda w: pl.BlockSpec((tm, w), lambda i: (i, 0))
    full = lambda a: pl.BlockSpec(a.shape, lambda i: (0,) * a.ndim, pipeline_mode=pl.Buffered(1))
    tab = pl.BlockSpec((tm, LANES), lambda i: (i % nt, 0))
    tr = pl.BlockSpec((None, DA_W, tm), lambda i: (i // nt, 0, i % nt))
    sds = jax.ShapeDtypeStruct
    out_shape = (
        sds((R, HG_W), BF16), sds((R, HG_W), BF16), sds((R, HG_W), BF16), sds((R, HG_W), BF16), sds((R, HG_W), BF16),
        sds((R, HG_W), F32), sds((R, HG_W), F32),
        sds((R, LRU_W), BF16), sds((R, LRU_W), BF16),
        sds((B, DA_W, S), BF16), sds((B, 2 * HEADS, S, LANES), F8), sds((B, DA_W, S), BF16), sds((R, DA_W), BF16),
    )
    khead = pl.BlockSpec((None, 2 * HEADS, tm, LANES), lambda i: (i // nt, 0, i % nt, 0))
    out_specs = (
        row(HG_W), row(HG_W), row(HG_W), row(HG_W), row(HG_W), row(HG_W), row(HG_W),
        row(LRU_W), row(LRU_W),
        tr, khead, tr, row(DA_W),
    )
    return pl.pallas_call(
        _inproj_kernel,
        grid=(R // tm,),
        in_specs=[row(D_MODEL), full(g), _layer_block(w_in, l), full(lb), tab, tab, tab],
        out_specs=out_specs,
        out_shape=out_shape,
        compiler_params=_cparams(("parallel",)),
        name="inproj",
    )(x2d, g, w_in, lb, cos_t, s1_t, s2_t)


def _split2(x):
    hi = x.astype(BF16)
    lo = (x - hi.astype(F32)).astype(BF16)
    return hi, lo


def _dot_t(a, b):
    return lax.dot_general(a, b, (((1,), (1,)), ((), ())), preferred_element_type=F32)


def _dot_a(a, b):
    return lax.dot_general(a, b, (((0,), (0,)), ((), ())), preferred_element_type=F32)


def _head_rows(x_bf, hm_ref):
    zeros = jnp.zeros((HG_BLK, LANES), BF16)
    rows = []
    for h in range(HEADS):
        half = x_bf[:, LANES * (h // 2):LANES * (h // 2 + 1)] * hm_ref[h % 2]
        rows.append(jnp.concatenate([half, zeros] if h < 2 else [zeros, half], axis=1))
    return jnp.concatenate(rows, axis=0)


def _hgrn_tile(q_ref, lf_ref, k_ref, v_ref, ltri_ref, esel_ref, gmask_ref, hm_ref, bd_ref, s_ref, o_ref, reverse):
    T = T_HG
    nblk = HG_CHUNK // HG_BLK
    q = q_ref[...].astype(F32)
    k = k_ref[...].astype(F32)
    v_bf = v_ref[...]
    lf = lf_ref[...]

    ltri = ltri_ref[...]
    c = sum(jnp.dot(ltri, piece, preferred_element_type=F32) for piece in _split2(lf))

    nb = T // HG_BLK
    half = HG_BLK // 2
    c4 = c.reshape(nb, 2, half, HG_W)
    ck4 = (c - jnp.log2(k)).reshape(nb, 2, half, HG_W)
    q4 = q.reshape(nb, 2, half, HG_W)
    far = 0 if reverse else 1
    dacc = jnp.zeros((T, HG_W), F32)
    dfar = jnp.zeros((T // 2, HG_W), F32)
    for jl in range(HG_BLK):
        cj = ck4[:, jl // half, jl % half:jl % half + 1, :][:, None]
        if jl // half == far:
            xj = q4[:, far:far + 1] * jnp.exp2(jnp.minimum(c4[:, far:far + 1] - cj, 0.0))
            dfar = dfar + jnp.dot(xj.reshape(T // 2, HG_W).astype(BF16), esel_ref[jl], preferred_element_type=F32)
        else:
            near = 1 - far
            x_near = q4[:, near:near + 1] * jnp.exp2(jnp.minimum(c4[:, near:near + 1] - cj, 0.0))
            x_far = q4[:, far:far + 1] * jnp.exp2(c4[:, far:far + 1] - cj)
            xj = jnp.concatenate([x_near, x_far] if far else [x_far, x_near], axis=1)
            dacc = dacc + jnp.dot(xj.reshape(T, HG_W).astype(BF16), esel_ref[jl], preferred_element_type=F32)
    d4 = dacc.reshape(nb, 2, half, HG_W)
    dfar4 = d4[:, far:far + 1] + dfar.reshape(nb, 1, half, HG_W)
    d4 = jnp.concatenate([d4[:, 0:1], dfar4] if far else [dfar4, d4[:, 1:2]], axis=1)
    dacc = d4.reshape(T, HG_W) * gmask_ref[...]
    yield

    bd = bd_ref[...]
    zpad64 = jnp.zeros((HG_CHUNK, HG_W), BF16)
    zcol = jnp.zeros((HG_BLK, LANES), F32)
    chunks = list(range(T // HG_CHUNK))
    per_chunk = {}
    for n in chunks:
        r0 = n * HG_CHUNK
        cn = c[r0:r0 + HG_CHUNK]
        qn = q[r0:r0 + HG_CHUNK]
        kn = k[r0:r0 + HG_CHUNK]
        vn = v_bf[r0:r0 + HG_CHUNK]
        ctot = cn[0:1] if reverse else cn[HG_CHUNK - 1:HG_CHUNK]
        qt = (qn * jnp.exp2(cn)).astype(BF16)
        kt = (kn * jnp.exp2(ctot - cn)).astype(BF16)
        upd = bd * _dot_a(vn, kt)
        vbd = jnp.concatenate([_head_rows(vn[HG_BLK * b:HG_BLK * (b + 1)], hm_ref) for b in range(nblk)], axis=0)
        a_rows = []
        for i in range(nblk):
            rows = slice(HG_BLK * i, HG_BLK * (i + 1))
            d_i = dacc[r0 + HG_BLK * i:r0 + HG_BLK * (i + 1)]
            others = list(range(i + 1, nblk)) if reverse else list(range(i))
            if not others:
                a_rows.append(d_i)
                continue
            ref_row = cn[HG_BLK * (i + 1):HG_BLK * (i + 1) + 1] if reverse else cn[HG_BLK * i - 1:HG_BLK * i]
            qh = (qn[rows] * jnp.exp2(cn[rows] - ref_row)).astype(BF16)
            pieces = []
            for b in others:
                brow = slice(HG_BLK * b, HG_BLK * (b + 1))
                kh = (kn[brow] * jnp.exp2(ref_row - cn[brow])).astype(BF16)
                pieces.append(_head_rows(kh, hm_ref))
            if len(pieces) % 2:
                pieces = ([zpad64] + pieces) if reverse else (pieces + [zpad64])
            r_i = _dot_t(qh, jnp.concatenate(pieces, axis=0))
            if r_i.shape[1] == LANES:
                r_i = jnp.concatenate([zcol, r_i] if reverse else [r_i, zcol], axis=1)
            a_rows.append(r_i + d_i)
        a_t = jnp.concatenate(a_rows, axis=0).astype(BF16)
        per_chunk[n] = (qt, jnp.exp2(ctot), upd, a_t, vbd)
    yield

    o_intra = {n: jnp.dot(per_chunk[n][3], per_chunk[n][4], preferred_element_type=F32) for n in chunks}
    s_t = s_ref[...]
    for n in (reversed(chunks) if reverse else chunks):
        qt, decay, upd, _, _ = per_chunk[n]
        r0 = n * HG_CHUNK
        o_ref[r0:r0 + HG_CHUNK, :] = _dot_t(qt, s_t.astype(BF16)) + o_intra[n]
        s_t = s_t * decay + upd
    s_ref[...] = s_t
    yield


def _hgrn_kernel(qf_ref, lff_ref, kf_ref, vf_ref, qb_ref, lfb_ref, kb_ref, vb_ref,
                 ltf_ref, ltb_ref, esel_ref, gmf_ref, gmb_ref, hm_ref, bd_ref,
                 of_ref, ob_ref, sf_ref, sb_ref):
    @pl.when(pl.program_id(1) == 0)
    def _():
        sf_ref[...] = jnp.zeros_like(sf_ref)
        sb_ref[...] = jnp.zeros_like(sb_ref)

    tiles = [
        _hgrn_tile(qf_ref, lff_ref, kf_ref, vf_ref, ltf_ref, esel_ref, gmf_ref, hm_ref, bd_ref, sf_ref, of_ref, False),
        _hgrn_tile(qb_ref, lfb_ref, kb_ref, vb_ref, ltb_ref, esel_ref, gmb_ref, hm_ref, bd_ref, sb_ref, ob_ref, True),
    ]
    for _ in range(3):
        for t in tiles:
            next(t)


@functools.lru_cache(maxsize=None)
def _hgrn_consts():
    T = T_HG
    t = np.arange(T)
    same_chunk = (t[:, None] // HG_CHUNK) == (t[None, :] // HG_CHUNK)
    ltf = (same_chunk & (t[None, :] <= t[:, None])).astype(np.float32)
    ltb = (same_chunk & (t[None, :] >= t[:, None])).astype(np.float32)
    lane = np.arange(HG_W)
    col = np.arange(HG_W)
    esel = np.zeros((HG_BLK, HG_W, HG_W), np.float32)
    grp = HEADS * HG_BLK
    for jl in range(HG_BLK):
        esel[jl] = ((lane[:, None] // HG_DK) == ((col[None, :] % grp) // HG_BLK)) & ((col[None, :] % HG_BLK) == jl)
    blk_of_row = (t % HG_CHUNK) // HG_BLK
    own = (col[None, :] // grp) == blk_of_row[:, None]
    jl_col = col[None, :] % HG_BLK
    gmf = (own & (jl_col <= (t % HG_BLK)[:, None])).astype(np.float32)
    gmb = (own & (jl_col >= (t % HG_BLK)[:, None])).astype(np.float32)
    hm = np.stack([(np.arange(LANES) < HG_DK), (np.arange(LANES) >= HG_DK)]).astype(np.float32)
    hm = np.broadcast_to(hm[:, None, :], (2, HG_BLK, LANES)).copy()
    bd = ((lane[:, None] // HG_DK) == (lane[None, :] // HG_DK)).astype(np.float32)
    return ltf, ltb, esel, gmf, gmb, hm, bd


def _hgrn(hq, lff, hkf, hv, lfb, hkb, B, S):
    T = T_HG
    nt = S // T
    ltf, ltb, esel, gmf, gmb, hm, bd = _hgrn_consts()
    consts = (jnp.asarray(ltf, BF16), jnp.asarray(ltb, BF16), jnp.asarray(esel, BF16),
              jnp.asarray(gmf, F32), jnp.asarray(gmb, F32), jnp.asarray(hm, BF16), jnp.asarray(bd, F32))
    r3 = lambda a: a.reshape(B, S, HG_W)
    fwd = pl.BlockSpec((None, T, HG_W), lambda b, i: (b, i, 0))
    bwd = pl.BlockSpec((None, T, HG_W), lambda b, i: (b, nt - 1 - i, 0))
    full = lambda a: pl.BlockSpec(a.shape, lambda b, i: (0,) * a.ndim, pipeline_mode=pl.Buffered(1))
    of, ob = pl.pallas_call(
        _hgrn_kernel,
        grid=(B, nt),
        in_specs=[fwd, fwd, fwd, fwd, bwd, bwd, bwd, bwd] + [full(a) for a in consts],
        out_specs=(fwd, bwd),
        out_shape=(jax.ShapeDtypeStruct((B, S, HG_W), F32), jax.ShapeDtypeStruct((B, S, HG_W), F32)),
        scratch_shapes=[pltpu.VMEM((HG_W, HG_W), F32), pltpu.VMEM((HG_W, HG_W), F32)],
        compiler_params=_cparams(("parallel", "arbitrary")),
        name="hgrn",
    )(r3(hq), r3(lff), r3(hkf), r3(hv), r3(hq), r3(lfb), r3(hkb), r3(hv), *consts)
    return of.reshape(B * S, HG_W), ob.reshape(B * S, HG_W)


LRU_HALF = LRU_W // 2
LRU_HALO = 16


SUBLANES = 8
LRU_SEG = 16
LRU_NSEG = T_LRU // LRU_SEG
LRU_PITCH = LRU_SEG + SUBLANES


def _scan_chunk(a, b, carry, reverse, a_s, b_s, h_s):
    T, C = a.shape
    L, P, G = LRU_SEG, LRU_PITCH, LRU_NSEG // SUBLANES
    outs, carries = [], []
    for slab in range(C // LANES):
        lanes = slice(LANES * slab, LANES * (slab + 1))
        for s in range(LRU_NSEG):
            a_s[slab, P * s:P * s + L, :] = a[L * s:L * (s + 1), lanes]
            b_s[slab, P * s:P * s + L, :] = b[L * s:L * (s + 1), lanes]
        h = [jnp.zeros((SUBLANES, LANES), F32)] * G
        p = [jnp.ones((SUBLANES, LANES), F32)] * G
        h_loc = [[None] * L for _ in range(G)]
        p_loc = [[None] * L for _ in range(G)]
        for v in (range(L - 1, -1, -1) if reverse else range(L)):
            for g in range(G):
                rows = pl.ds(P * SUBLANES * g + v, SUBLANES, stride=P)
                av = a_s[slab, rows, :]
                h[g] = av * h[g] + b_s[slab, rows, :]
                p[g] = av * p[g]
                h_loc[g][v], p_loc[g][v] = h[g], p[g]
        c = carry[:, lanes]
        inflow = [None] * LRU_NSEG
        for s in (range(LRU_NSEG - 1, -1, -1) if reverse else range(LRU_NSEG)):
            g, r = divmod(s, SUBLANES)
            inflow[s] = c
            c = p[g][r:r + 1] * c + h[g][r:r + 1]
        for g in range(G):
            cin = jnp.concatenate(inflow[SUBLANES * g:SUBLANES * (g + 1)], axis=0)
            for v in range(L):
                h_s[slab, pl.ds(P * SUBLANES * g + v, SUBLANES, stride=P), :] = h_loc[g][v] + p_loc[g][v] * cin
        outs.append(jnp.concatenate([h_s[slab, P * s:P * s + L, :] for s in range(LRU_NSEG)], axis=0))
        carries.append(c)
    return jnp.concatenate(outs, axis=1), jnp.concatenate(carries, axis=1)


def _lru_kernel(lx_ref, lg_ref, cw_ref, cb_ref, w_ref, bias_ref, cneg_ref, out_ref, u_ref, ub_ref, hf_ref, hb_ref,
                *stage_refs, S):
    T = T_LRU
    n = S // T
    C = LRU_HALF
    cw = cw_ref[...]
    cb = cb_ref[...]

    def conv_body(i, _):
        t0 = pl.multiple_of(i * T, T)
        cur = lx_ref[pl.ds(t0, T), :].astype(F32)
        ps = pl.multiple_of(jnp.maximum(t0 - LRU_HALO, 0), LRU_HALO)
        prev = jnp.where(i > 0, lx_ref[pl.ds(ps, LRU_HALO), :].astype(F32), 0.0)
        ns = pl.multiple_of(jnp.minimum(t0 + T, S - LRU_HALO), LRU_HALO)
        nxt = jnp.where(i < n - 1, lx_ref[pl.ds(ns, LRU_HALO), :].astype(F32), 0.0)
        ext = jnp.concatenate([prev, cur, nxt], axis=0)
        h = LRU_HALO
        u = (cb + ext[h - 2:h - 2 + T] * cw[0:1] + ext[h - 1:h - 1 + T] * cw[1:2]
             + cur * cw[2:3] + ext[h + 1:h + 1 + T] * cw[3:4])
        u_ref[pl.ds(t0, T), :] = u
        ub_ref[pl.ds(t0, T), :] = u.astype(BF16)
        return 0

    lax.fori_loop(0, n, conv_body, 0)

    def gates(d, i):
        rows = pl.ds(pl.multiple_of(i * T, T), T)
        u = u_ref[rows, :]
        pre = jnp.dot(ub_ref[rows, :], w_ref[:, 2 * C * d:2 * C * (d + 1)], preferred_element_type=F32)
        r = _sigmoid(pre[:, :C] + bias_ref[2 * d:2 * d + 1, :])
        ig = _sigmoid(pre[:, C:] + bias_ref[2 * d + 1:2 * d + 2, :])
        a = jnp.exp2(cneg_ref[d:d + 1, :] * r)
        y = 1.0 - a * a
        b = y * lax.rsqrt(jnp.maximum(y, F32_TINY)) * (ig * u)
        return a, b

    def emit(rows, h):
        g = lg_ref[rows, :].astype(F32)
        out_ref[rows, :] = (h * (g * _sigmoid(g))).astype(BF16)

    def step(i, carry, meet):
        cf, cr = carry
        j = n - 1 - i
        rows_i = pl.ds(pl.multiple_of(i * T, T), T)
        rows_j = pl.ds(pl.multiple_of(j * T, T), T)
        af, bf = gates(0, i)
        ar, br = gates(1, j)
        hf, cf = _scan_chunk(af, bf, cf, False, *stage_refs[0:3])
        hr, cr = _scan_chunk(ar, br, cr, True, *stage_refs[3:6])
        if meet:
            emit(rows_i, hf + hb_ref[rows_i, :])
            emit(rows_j, hf_ref[rows_j, :] + hr)
        else:
            hf_ref[rows_i, :] = hf
            hb_ref[rows_j, :] = hr
        return cf, cr

    zero = jnp.zeros((1, C), F32)
    unroll = 2 if n % 4 == 0 else 1
    carry = lax.fori_loop(0, n // 2, functools.partial(step, meet=False), (zero, zero), unroll=unroll)
    lax.fori_loop(n // 2, n, functools.partial(step, meet=True), carry, unroll=unroll)


def _lru(lx, lg, B, S, cw, cb, w, bias, cneg):
    C = LRU_HALF
    seq = pl.BlockSpec((None, S, C), lambda b, c: (b, 0, c))
    vec = lambda r: pl.BlockSpec((r, C), lambda b, c: (0, c))
    out = pl.pallas_call(
        functools.partial(_lru_kernel, S=S),
        grid=(B, LRU_W // C),
        in_specs=[seq, seq, vec(4), vec(1), pl.BlockSpec((None, C, 4 * C), lambda b, c: (c, 0, 0)), vec(4), vec(2)],
        out_specs=seq,
        out_shape=jax.ShapeDtypeStruct((B, S, LRU_W), BF16),
        scratch_shapes=[pltpu.VMEM((S, C), F32), pltpu.VMEM((S, C), BF16), pltpu.VMEM((S, C), F32), pltpu.VMEM((S, C), F32)]
        + [pltpu.VMEM((C // LANES, LRU_NSEG * LRU_PITCH, LANES), F32)] * 6,
        compiler_params=_cparams(("parallel", "parallel")),
        name="lru",
    )(lx.reshape(B, S, LRU_W), lg.reshape(B, S, LRU_W), cw, cb, w, bias, cneg)
    return out.reshape(B * S, LRU_W)


PV_ROWS = DA_V + 16
M_INIT = -1e30
ATTN_GROUP = 2


def _attn_kernel(lam_ref, qT_ref, qTn_ref, k_ref, vT_ref, dg_ref, ng_ref, out_ref, rhs_ref, m_ref, acc_ref, sa_ref, sb_ref,
                 *, S, tk, lam_init):
    nk = S // tk
    qi = pl.program_id(1)
    cur_q = qi % 2
    nxt_q = 1 - cur_q
    ones = jnp.ones((PV_ROWS - DA_V, tk), BF16)

    def make_rhs(q_ref, slot):
        nslot = LANES // DA_QK
        for hc in range(2 * HEADS):
            qq = q_ref[DA_QK * hc:DA_QK * (hc + 1), :].astype(F32)
            hi = qq.astype(F8)
            lo = (qq - hi.astype(F32)).astype(F8)
            pieces = [hi, hi, lo, lo]
            rhs_ref[slot, hc] = jnp.concatenate([pieces[(s - hc) % nslot] for s in range(nslot)], axis=0)

    def stage(j_next, q_slot, next_ref, j_cur, cur_ref):
        krows = pl.ds(pl.multiple_of(j_next * tk, tk), tk)
        if j_cur is not None:
            vT = vT_ref[:, pl.ds(pl.multiple_of(j_cur * tk, tk), tk)]
        for hc in range(2 * HEADS):
            if hc % ATTN_GROUP == 0:
                for g in range(hc, hc + ATTN_GROUP):
                    next_ref[g] = jnp.dot(k_ref[g, krows, :], rhs_ref[q_slot, g], preferred_element_type=F32)
            if j_cur is not None:
                h = hc // 2
                s = cur_ref[hc]
                m_old = m_ref[hc]
                m_new = jnp.maximum(m_old, jnp.max(s, axis=0, keepdims=True))
                p = jnp.exp2(s - m_new).astype(BF16)
                lhs = jnp.concatenate([vT[DA_V * h:DA_V * (h + 1), :], ones], axis=0)
                acc_ref[hc] = jnp.exp2(m_old - m_new) * acc_ref[hc] + jnp.dot(lhs, p, preferred_element_type=F32)
                m_ref[hc] = m_new

    @pl.when(qi == 0)
    def _():
        make_rhs(qT_ref, 0)
        stage(0, 0, sa_ref, None, None)

    make_rhs(qTn_ref, nxt_q)
    m_ref[...] = jnp.full(m_ref.shape, M_INIT, F32)
    acc_ref[...] = jnp.zeros_like(acc_ref)

    def body(i, _):
        stage(2 * i + 1, cur_q, sb_ref, 2 * i, sa_ref)
        stage(2 * i + 2, cur_q, sa_ref, 2 * i + 1, sb_ref)
        return 0

    lax.fori_loop(0, nk // 2 - 1, body, 0)
    stage(nk - 1, cur_q, sb_ref, nk - 2, sa_ref)
    stage(0, nxt_q, sa_ref, nk - 1, sb_ref)

    lam = lam_ref[0]
    outs = []
    for h in range(HEADS):
        a0 = acc_ref[2 * h]
        a1 = acc_ref[2 * h + 1]
        o = a0[:DA_V] * (1.0 / a0[DA_V:DA_V + 1]) - lam * (a1[:DA_V] * (1.0 / a1[DA_V:DA_V + 1]))
        ms = jnp.mean(o * o, axis=0, keepdims=True)
        outs.append(o * lax.rsqrt(ms + RMS_EPS))
    o = jnp.concatenate(outs, axis=0).T
    g = dg_ref[...].astype(F32)
    out_ref[...] = (o * ng_ref[...] * (1.0 - lam_init) * (g * _sigmoid(g))).astype(BF16)


def _attn(lam, dqT, dk, dvT, dg, ng, B, S, lam_init):
    nq = S // TQ
    tk = min(TK, S // 2)
    out = pl.pallas_call(
        functools.partial(_attn_kernel, S=S, tk=tk, lam_init=lam_init),
        grid=(B, nq),
        in_specs=[
            pl.BlockSpec(memory_space=pltpu.SMEM),
            pl.BlockSpec((None, DA_W, TQ), lambda b, i: (b, 0, i)),
            pl.BlockSpec((None, DA_W, TQ), lambda b, i: (b, 0, jnp.minimum(i + 1, nq - 1))),
            pl.BlockSpec((None, 2 * HEADS, S, LANES), lambda b, i: (b, 0, 0, 0)),
            pl.BlockSpec((None, DA_W, S), lambda b, i: (b, 0, 0)),
            pl.BlockSpec((None, TQ, DA_W), lambda b, i: (b, i, 0)),
            pl.BlockSpec((1, DA_W), lambda b, i: (0, 0)),
        ],
        out_specs=pl.BlockSpec((None, TQ, DA_W), lambda b, i: (b, i, 0)),
        out_shape=jax.ShapeDtypeStruct((B, S, DA_W), BF16),
        scratch_shapes=[
            pltpu.VMEM((2, 2 * HEADS, LANES, TQ), F8),
            pltpu.VMEM((2 * HEADS, 1, TQ), F32),
            pltpu.VMEM((2 * HEADS, PV_ROWS, TQ), F32),
            pltpu.VMEM((2 * HEADS, tk, TQ), F32),
            pltpu.VMEM((2 * HEADS, tk, TQ), F32),
        ],
        compiler_params=_cparams(("parallel", "arbitrary")),
        name="attn",
    )(lam, dqT, dqT, dk, dvT, dg.reshape(B, S, DA_W), ng)
    return out.reshape(B * S, DA_W)


def _outproj_kernel(h_ref, of_ref, ob_ref, hg_ref, olru_ref, oda_ref, p_ref, hgn_ref, ones_ref,
                    wout_ref, gw_ref, pw_ref, fn_ref, out_ref, *, final):
    acc = h_ref[...]
    acc = acc + jnp.dot(olru_ref[...], wout_ref[HG_W:HG_W + LRU_W, :], preferred_element_type=F32)
    acc = acc + jnp.dot(oda_ref[...], wout_ref[HG_W + LRU_W:, :], preferred_element_type=F32)
    ple = jnp.dot(p_ref[...].astype(BF16), pw_ref[...], preferred_element_type=F32)

    o = of_ref[...] + ob_ref[...]
    sq = o * o
    hi = sq.astype(BF16)
    lo = (sq - hi.astype(F32)).astype(BF16)
    ones = ones_ref[...]
    ssum = jnp.dot(hi, ones, preferred_element_type=F32) + jnp.dot(lo, ones, preferred_element_type=F32)
    g = hg_ref[...].astype(F32)
    ohg = (o * lax.rsqrt(ssum * (1.0 / HG_DK) + RMS_EPS) * hgn_ref[...] * (g * _sigmoid(g))).astype(BF16)

    acc = acc + jnp.dot(ohg, wout_ref[0:HG_W, :], preferred_element_type=F32)
    gate = _sigmoid(jnp.dot(acc.astype(BF16), gw_ref[...], preferred_element_type=F32))
    h2 = acc + gate * ple
    if final:
        ms = jnp.mean(h2 * h2, axis=-1, keepdims=True)
        h2 = h2 * lax.rsqrt(ms + RMS_EPS) * fn_ref[...]
    out_ref[...] = h2


def _outproj(h, of, ob, hg, olru, oda, p, l, hgn, ones, wout, gw, pw, fn, final):
    R = h.shape[0]
    tm = min(TM_OUT, R)
    row = lambda w: pl.BlockSpec((tm, w), lambda i: (i, 0))
    full = lambda a: pl.BlockSpec(a.shape, lambda i: (0,) * a.ndim, pipeline_mode=pl.Buffered(1))
    return pl.pallas_call(
        functools.partial(_outproj_kernel, final=final),
        grid=(R // tm,),
        in_specs=[row(D_MODEL), row(HG_W), row(HG_W), row(HG_W), row(LRU_W), row(DA_W),
                  pl.BlockSpec((None, tm, PLE_DIM), lambda i: (l, i, 0)),
                  full(hgn), full(ones), _layer_block(wout, l), _layer_block(gw, l), _layer_block(pw, l), full(fn)],
        out_specs=row(D_MODEL),
        out_shape=jax.ShapeDtypeStruct((R, D_MODEL), F32),
        compiler_params=_cparams(("parallel",)),
        name="outproj",
    )(h, of, ob, hg, olru, oda, p, hgn, ones, wout, gw, pw, fn)


def _rope_tables(S):
    half = ROPE_DIM // 2
    inv = ROPE_THETA ** (-jnp.arange(half, dtype=F32) * 2.0 / ROPE_DIM)
    ang = jnp.arange(S, dtype=F32)[:, None] * inv[None, :]
    cos, sin = jnp.cos(ang), jnp.sin(ang)
    ones = jnp.ones((S, DA_QK - ROPE_DIM), F32)
    zeros = jnp.zeros((S, DA_QK - ROPE_DIM), F32)
    z4 = jnp.zeros((S, half), F32)
    rep = lambda t: jnp.tile(t, (1, LANES // DA_QK))
    return (rep(jnp.concatenate([cos, cos, ones], axis=1)),
            rep(jnp.concatenate([-sin, z4, zeros], axis=1)),
            rep(jnp.concatenate([z4, sin, zeros], axis=1)))


def _lru_gate_weights(wa, wx):
    per_half = LRU_HALF // LRU_BLOCK
    halves = []
    for c in range(LRU_W // LRU_HALF):
        tiles = []
        for d in range(2):
            for w in (wa, wx):
                blocks = [w[d, per_half * c + m] for m in range(per_half)]
                tiles.append(jax.scipy.linalg.block_diag(*blocks))
        halves.append(jnp.concatenate(tiles, axis=1))
    return jnp.stack(halves).astype(BF16)


def _trunk(x, p, B, S, prm):
    R = B * S
    h = x.reshape(R, D_MODEL)
    depth = prm["w_in"].shape[0]
    cos_t, s1_t, s2_t = _rope_tables(S)
    lb_all = jnp.cumsum(jax.nn.softmax(prm["hg_lb"].astype(F32), axis=0), axis=0)
    lb_all = lb_all - lb_all[:1]
    lane = np.arange(HG_W)
    ones_bd = jnp.asarray((lane[:, None] // HG_DK) == (lane[None, :] // HG_DK), BF16)
    p3 = p.reshape(depth, R, PLE_DIM)
    for l in range(depth):
        (hq, hkf, hkb, hv, hg, lff, lfb, lx, lg, dqT, dk, dvT, dg) = _inproj(
            h, B, S, l, prm["norm_g"][l][None, :], prm["w_in_bf"], lb_all[l], cos_t, s1_t, s2_t)

        of, ob = _hgrn(hq, lff, hkf, hv, lfb, hkb, B, S)

        cneg = -LRU_C * LOG2E * jax.nn.softplus(-prm["lru_lam"][l].astype(F32))
        bias = jnp.stack([prm["lru_ba"][l, 0], prm["lru_bx"][l, 0], prm["lru_ba"][l, 1], prm["lru_bx"][l, 1]])
        olru = _lru(lx, lg, B, S, prm["lru_conv_w"][l], prm["lru_conv_b"][l][None, :],
                    _lru_gate_weights(prm["lru_wa"][l], prm["lru_wx"][l]), bias, cneg)

        lam_init = 0.8 - 0.6 * math.exp(-0.3 * l)
        lam = (jnp.exp(jnp.sum(prm["da_lq1"][l].astype(F32) * prm["da_lk1"][l].astype(F32)))
               - jnp.exp(jnp.sum(prm["da_lq2"][l].astype(F32) * prm["da_lk2"][l].astype(F32))) + lam_init)
        oda = _attn(lam.reshape(1), dqT, dk, dvT, dg, jnp.tile(prm["da_norm"][l], HEADS)[None, :], B, S, lam_init)

        h = _outproj(h, of, ob, hg, olru, oda, p3, l, jnp.tile(prm["hg_norm"][l], HEADS)[None, :],
                     ones_bd, prm["w_out_bf"], prm["ple_gate_w_bf"], prm["ple_w_bf"], prm["final_norm"][None, :],
                     final=(l == depth - 1))
    return h.reshape(B, S, D_MODEL)


def kernel(x_prompt, x_sample, p_prompt, p_sample, norm_g, w_in, w_out, hg_lb, hg_norm, lru_conv_w, lru_conv_b,
           lru_wa, lru_ba, lru_wx, lru_bx, lru_lam, da_lq1, da_lk1, da_lq2, da_lk2, da_norm, ple_w, ple_gate_w,
           final_norm):
    prm = dict(norm_g=norm_g, w_in=w_in, w_out=w_out, hg_lb=hg_lb, hg_norm=hg_norm, lru_conv_w=lru_conv_w,
               lru_conv_b=lru_conv_b, lru_wa=lru_wa, lru_ba=lru_ba, lru_wx=lru_wx, lru_bx=lru_bx, lru_lam=lru_lam,
               da_lq1=da_lq1, da_lk1=da_lk1, da_lq2=da_lq2, da_lk2=da_lk2, da_norm=da_norm, ple_w=ple_w,
               ple_gate_w=ple_gate_w, final_norm=final_norm)
    prm.update(w_in_bf=w_in.astype(BF16), w_out_bf=w_out.astype(BF16), ple_gate_w_bf=ple_gate_w.astype(BF16),
               ple_w_bf=ple_w.astype(BF16))
    outs = []
    for x, p in ((x_prompt, p_prompt), (x_sample, p_sample)):
        B, S, _ = x.shape
        outs.append(_trunk(x, p, B, S, prm))
    return tuple(outs)
```

```python
import functools
import math

import numpy as np
import jax
import jax.numpy as jnp
from jax import lax
from jax.experimental import pallas as pl
from jax.experimental.pallas import tpu as pltpu

F32 = jnp.float32
BF16 = jnp.bfloat16
F8 = jnp.float8_e4m3fn

D_MODEL = 1024
PLE_DIM = 256
HEADS = 4
HG_W = 256
HG_DK = HG_W // HEADS
LRU_W = 512
LRU_BLOCK = 64
LRU_C = 8.0
DA_W = 256
DA_QK = 32
DA_V = 64
ROPE_THETA = 500000.0
ROPE_DIM = 8
RMS_EPS = 1e-6
LOG2E = 1.4426950408889634
F32_TINY = float(np.finfo(np.float32).tiny)
D_HG_IN = 5 * HG_W
D_LRU_IN = 2 * LRU_W
D_DA_IN = 4 * DA_W

LANES = 128
VMEM_LIMIT = 56 * 1024 * 1024

TM_IN = 1024
TM_OUT = 1024
T_HG = 256
HG_CHUNK = 64
HG_BLK = 16
T_LRU = 256
TQ = 256
TK = 512


def _sigmoid(x):
    return 1.0 / (1.0 + jnp.exp2(x * -LOG2E))


def _cparams(sem):
    return pltpu.CompilerParams(dimension_semantics=sem, vmem_limit_bytes=VMEM_LIMIT)


def _rope(t, cos, s1, s2):
    halves = []
    for j in range(2):
        th = t[:, LANES * j:LANES * (j + 1)]
        up = pltpu.roll(th, LANES - ROPE_DIM // 2, 1)
        dn = pltpu.roll(th, ROPE_DIM // 2, 1)
        halves.append(th * cos + up * s1 + dn * s2)
    return jnp.concatenate(halves, axis=1)


def _inproj_kernel(x_ref, g_ref, w_ref, lb_ref, cos_ref, s1_ref, s2_ref,
                   hq_ref, hkf_ref, hkb_ref, hv_ref, hg_ref, lff_ref, lfb_ref, lx_ref, lg_ref,
                   dqT_ref, dk_ref, dvT_ref, dg_ref):
    x = x_ref[...]
    ms = jnp.mean(x * x, axis=-1, keepdims=True)
    hn = (x * lax.rsqrt(ms + RMS_EPS) * g_ref[...]).astype(BF16)

    z = jnp.dot(hn, w_ref[:, D_HG_IN + D_LRU_IN:], preferred_element_type=F32)
    cos, s1, s2 = cos_ref[...], s1_ref[...], s2_ref[...]
    q = _rope(z[:, 0:DA_W], cos, s1, s2) * (DA_QK ** -0.5 * LOG2E)
    k = _rope(z[:, DA_W:2 * DA_W], cos, s1, s2)
    dqT_ref[...] = q.T.astype(BF16)
    slot = lax.broadcasted_iota(jnp.int32, (k.shape[0], LANES), 1) // DA_QK
    for half in range(DA_W // LANES):
        kk = k[:, LANES * half:LANES * (half + 1)]
        hi = kk.astype(F8).astype(F32)
        lo = kk - hi
        r1 = pltpu.roll(lo, DA_QK, 1)
        r2 = pltpu.roll(hi, 2 * DA_QK, 1)
        r3 = pltpu.roll(lo, 3 * DA_QK, 1)
        for j in range(LANES // DA_QK):
            s = (slot - j) & (LANES // DA_QK - 1)
            row = jnp.where(s == 0, hi, jnp.where(s == 1, r1, jnp.where(s == 2, r2, r3)))
            dk_ref[(LANES // DA_QK) * half + j] = row.astype(F8)
    dvT_ref[...] = z[:, 2 * DA_W:3 * DA_W].T.astype(BF16)
    dg_ref[...] = z[:, 3 * DA_W:4 * DA_W].astype(BF16)

    z = jnp.dot(hn, w_ref[:, 0:D_HG_IN], preferred_element_type=F32)
    zq = z[:, 0:HG_W]
    hq_ref[...] = (zq * _sigmoid(zq)).astype(BF16)
    for d, lf_ref, k_ref in ((0, lff_ref, hkf_ref), (1, lfb_ref, hkb_ref)):
        zf = z[:, HG_W * (1 + d):HG_W * (2 + d)]
        lb = lb_ref[d:d + 1, :]
        s = _sigmoid(zf)
        lf_ref[...] = jnp.log2(lb + (1.0 - lb) * s)
        k_ref[...] = ((1.0 - lb) * (1.0 - s)).astype(BF16)
    hv_ref[...] = z[:, 3 * HG_W:4 * HG_W].astype(BF16)
    hg_ref[...] = z[:, 4 * HG_W:5 * HG_W].astype(BF16)

    z = jnp.dot(hn, w_ref[:, D_HG_IN:D_HG_IN + D_LRU_IN], preferred_element_type=F32)
    lx_ref[...] = z[:, :LRU_W].astype(BF16)
    lg_ref[...] = z[:, LRU_W:].astype(BF16)


def _layer_block(a, l):
    return pl.BlockSpec((None,) + a.shape[1:], lambda i: (l,) + (0,) * (a.ndim - 1), pipeline_mode=pl.Buffered(1))


def _inproj(x2d, B, S, l, g, w_in, lb, cos_t, s1_t, s2_t):
    R = B * S
    tm = min(TM_IN, S)
    nt = S // tm
    row = lambda w: pl.BlockSpec((tm, w), lambda i: (i, 0))
    full = lambda a: pl.BlockSpec(a.shape, lambda i: (0,) * a.ndim, pipeline_mode=pl.Buffered(1))
    tab = pl.BlockSpec((tm, LANES), lambda i: (i % nt, 0))
    tr = pl.BlockSpec((None, DA_W, tm), lambda i: (i // nt, 0, i % nt))
    sds = jax.ShapeDtypeStruct
    out_shape = (
        sds((R, HG_W), BF16), sds((R, HG_W), BF16), sds((R, HG_W), BF16), sds((R, HG_W), BF16), sds((R, HG_W), BF16),
        sds((R, HG_W), F32), sds((R, HG_W), F32),
        sds((R, LRU_W), BF16), sds((R, LRU_W), BF16),
        sds((B, DA_W, S), BF16), sds((B, 2 * HEADS, S, LANES), F8), sds((B, DA_W, S), BF16), sds((R, DA_W), BF16),
    )
    khead = pl.BlockSpec((None, 2 * HEADS, tm, LANES), lambda i: (i // nt, 0, i % nt, 0))
    out_specs = (
        row(HG_W), row(HG_W), row(HG_W), row(HG_W), row(HG_W), row(HG_W), row(HG_W),
        row(LRU_W), row(LRU_W),
        tr, khead, tr, row(DA_W),
    )
    return pl.pallas_call(
        _inproj_kernel,
        grid=(R // tm,),
        in_specs=[row(D_MODEL), full(g), _layer_block(w_in, l), full(lb), tab, tab, tab],
        out_specs=out_specs,
        out_shape=out_shape,
        compiler_params=_cparams(("parallel",)),
        name="inproj",
    )(x2d, g, w_in, lb, cos_t, s1_t, s2_t)


def _split2(x):
    hi = x.astype(BF16)
    lo = (x - hi.astype(F32)).astype(BF16)
    return hi, lo


def _dot_t(a, b):
    return lax.dot_general(a, b, (((1,), (1,)), ((), ())), preferred_element_type=F32)


def _dot_a(a, b):
    return lax.dot_general(a, b, (((0,), (0,)), ((), ())), preferred_element_type=F32)


def _head_rows(x_bf, hm_ref):
    zeros = jnp.zeros((HG_BLK, LANES), BF16)
    rows = []
    for h in range(HEADS):
        half = x_bf[:, LANES * (h // 2):LANES * (h // 2 + 1)] * hm_ref[h % 2]
        rows.append(jnp.concatenate([half, zeros] if h < 2 else [zeros, half], axis=1))
    return jnp.concatenate(rows, axis=0)


def _hgrn_tile(q_ref, lf_ref, k_ref, v_ref, ltri_ref, esel_ref, gmask_ref, hm_ref, bd_ref, s_ref, o_ref, reverse):
    T = T_HG
    nblk = HG_CHUNK // HG_BLK
    q = q_ref[...].astype(F32)
    k = k_ref[...].astype(F32)
    v_bf = v_ref[...]
    lf = lf_ref[...]

    ltri = ltri_ref[...]
    c = sum(jnp.dot(ltri, piece, preferred_element_type=F32) for piece in _split2(lf))

    nb = T // HG_BLK
    half = HG_BLK // 2
    c4 = c.reshape(nb, 2, half, HG_W)
    ck4 = (c - jnp.log2(k)).reshape(nb, 2, half, HG_W)
    q4 = q.reshape(nb, 2, half, HG_W)
    far = 0 if reverse else 1
    dacc = jnp.zeros((T, HG_W), F32)
    dfar = jnp.zeros((T // 2, HG_W), F32)
    for jl in range(HG_BLK):
        cj = ck4[:, jl // half, jl % half:jl % half + 1, :][:, None]
        if jl // half == far:
            xj = q4[:, far:far + 1] * jnp.exp2(jnp.minimum(c4[:, far:far + 1] - cj, 0.0))
            dfar = dfar + jnp.dot(xj.reshape(T // 2, HG_W).astype(BF16), esel_ref[jl], preferred_element_type=F32)
        else:
            near = 1 - far
            x_near = q4[:, near:near + 1] * jnp.exp2(jnp.minimum(c4[:, near:near + 1] - cj, 0.0))
            x_far = q4[:, far:far + 1] * jnp.exp2(c4[:, far:far + 1] - cj)
            xj = jnp.concatenate([x_near, x_far] if far else [x_far, x_near], axis=1)
            dacc = dacc + jnp.dot(xj.reshape(T, HG_W).astype(BF16), esel_ref[jl], preferred_element_type=F32)
    d4 = dacc.reshape(nb, 2, half, HG_W)
    dfar4 = d4[:, far:far + 1] + dfar.reshape(nb, 1, half, HG_W)
    d4 = jnp.concatenate([d4[:, 0:1], dfar4] if far else [dfar4, d4[:, 1:2]], axis=1)
    dacc = d4.reshape(T, HG_W) * gmask_ref[...]
    yield

    bd = bd_ref[...]
    zpad64 = jnp.zeros((HG_CHUNK, HG_W), BF16)
    zcol = jnp.zeros((HG_BLK, LANES), F32)
    chunks = list(range(T // HG_CHUNK))
    per_chunk = {}
    for n in chunks:
        r0 = n * HG_CHUNK
        cn = c[r0:r0 + HG_CHUNK]
        qn = q[r0:r0 + HG_CHUNK]
        kn = k[r0:r0 + HG_CHUNK]
        vn = v_bf[r0:r0 + HG_CHUNK]
        ctot = cn[0:1] if reverse else cn[HG_CHUNK - 1:HG_CHUNK]
        qt = (qn * jnp.exp2(cn)).astype(BF16)
        kt = (kn * jnp.exp2(ctot - cn)).astype(BF16)
        upd = bd * _dot_a(vn, kt)
        vbd = jnp.concatenate([_head_rows(vn[HG_BLK * b:HG_BLK * (b + 1)], hm_ref) for b in range(nblk)], axis=0)
        a_rows = []
        for i in range(nblk):
            rows = slice(HG_BLK * i, HG_BLK * (i + 1))
            d_i = dacc[r0 + HG_BLK * i:r0 + HG_BLK * (i + 1)]
            others = list(range(i + 1, nblk)) if reverse else list(range(i))
            if not others:
                a_rows.append(d_i)
                continue
            ref_row = cn[HG_BLK * (i + 1):HG_BLK * (i + 1) + 1] if reverse else cn[HG_BLK * i - 1:HG_BLK * i]
            qh = (qn[rows] * jnp.exp2(cn[rows] - ref_row)).astype(BF16)
            pieces = []
            for b in others:
                brow = slice(HG_BLK * b, HG_BLK * (b + 1))
                kh = (kn[brow] * jnp.exp2(ref_row - cn[brow])).astype(BF16)
                pieces.append(_head_rows(kh, hm_ref))
            if len(pieces) % 2:
                pieces = ([zpad64] + pieces) if reverse else (pieces + [zpad64])
            r_i = _dot_t(qh, jnp.concatenate(pieces, axis=0))
            if r_i.shape[1] == LANES:
                r_i = jnp.concatenate([zcol, r_i] if reverse else [r_i, zcol], axis=1)
            a_rows.append(r_i + d_i)
        a_t = jnp.concatenate(a_rows, axis=0).astype(BF16)
        per_chunk[n] = (qt, jnp.exp2(ctot), upd, a_t, vbd)
    yield

    o_intra = {n: jnp.dot(per_chunk[n][3], per_chunk[n][4], preferred_element_type=F32) for n in chunks}
    s_t = s_ref[...]
    for n in (reversed(chunks) if reverse else chunks):
        qt, decay, upd, _, _ = per_chunk[n]
        r0 = n * HG_CHUNK
        o_ref[r0:r0 + HG_CHUNK, :] = _dot_t(qt, s_t.astype(BF16)) + o_intra[n]
        s_t = s_t * decay + upd
    s_ref[...] = s_t
    yield


def _hgrn_kernel(qf_ref, lff_ref, kf_ref, vf_ref, qb_ref, lfb_ref, kb_ref, vb_ref,
                 ltf_ref, ltb_ref, esel_ref, gmf_ref, gmb_ref, hm_ref, bd_ref,
                 of_ref, ob_ref, sf_ref, sb_ref):
    @pl.when(pl.program_id(1) == 0)
    def _():
        sf_ref[...] = jnp.zeros_like(sf_ref)
        sb_ref[...] = jnp.zeros_like(sb_ref)

    tiles = [
        _hgrn_tile(qf_ref, lff_ref, kf_ref, vf_ref, ltf_ref, esel_ref, gmf_ref, hm_ref, bd_ref, sf_ref, of_ref, False),
        _hgrn_tile(qb_ref, lfb_ref, kb_ref, vb_ref, ltb_ref, esel_ref, gmb_ref, hm_ref, bd_ref, sb_ref, ob_ref, True),
    ]
    for _ in range(3):
        for t in tiles:
            next(t)


@functools.lru_cache(maxsize=None)
def _hgrn_consts():
    T = T_HG
    t = np.arange(T)
    same_chunk = (t[:, None] // HG_CHUNK) == (t[None, :] // HG_CHUNK)
    ltf = (same_chunk & (t[None, :] <= t[:, None])).astype(np.float32)
    ltb = (same_chunk & (t[None, :] >= t[:, None])).astype(np.float32)
    lane = np.arange(HG_W)
    col = np.arange(HG_W)
    esel = np.zeros((HG_BLK, HG_W, HG_W), np.float32)
    grp = HEADS * HG_BLK
    for jl in range(HG_BLK):
        esel[jl] = ((lane[:, None] // HG_DK) == ((col[None, :] % grp) // HG_BLK)) & ((col[None, :] % HG_BLK) == jl)
    blk_of_row = (t % HG_CHUNK) // HG_BLK
    own = (col[None, :] // grp) == blk_of_row[:, None]
    jl_col = col[None, :] % HG_BLK
    gmf = (own & (jl_col <= (t % HG_BLK)[:, None])).astype(np.float32)
    gmb = (own & (jl_col >= (t % HG_BLK)[:, None])).astype(np.float32)
    hm = np.stack([(np.arange(LANES) < HG_DK), (np.arange(LANES) >= HG_DK)]).astype(np.float32)
    hm = np.broadcast_to(hm[:, None, :], (2, HG_BLK, LANES)).copy()
    bd = ((lane[:, None] // HG_DK) == (lane[None, :] // HG_DK)).astype(np.float32)
    return ltf, ltb, esel, gmf, gmb, hm, bd


def _hgrn(hq, lff, hkf, hv, lfb, hkb, B, S):
    T = T_HG
    nt = S // T
    ltf, ltb, esel, gmf, gmb, hm, bd = _hgrn_consts()
    consts = (jnp.asarray(ltf, BF16), jnp.asarray(ltb, BF16), jnp.asarray(esel, BF16),
              jnp.asarray(gmf, F32), jnp.asarray(gmb, F32), jnp.asarray(hm, BF16), jnp.asarray(bd, F32))
    r3 = lambda a: a.reshape(B, S, HG_W)
    fwd = pl.BlockSpec((None, T, HG_W), lambda b, i: (b, i, 0))
    bwd = pl.BlockSpec((None, T, HG_W), lambda b, i: (b, nt - 1 - i, 0))
    full = lambda a: pl.BlockSpec(a.shape, lambda b, i: (0,) * a.ndim, pipeline_mode=pl.Buffered(1))
    of, ob = pl.pallas_call(
        _hgrn_kernel,
        grid=(B, nt),
        in_specs=[fwd, fwd, fwd, fwd, bwd, bwd, bwd, bwd] + [full(a) for a in consts],
        out_specs=(fwd, bwd),
        out_shape=(jax.ShapeDtypeStruct((B, S, HG_W), F32), jax.ShapeDtypeStruct((B, S, HG_W), F32)),
        scratch_shapes=[pltpu.VMEM((HG_W, HG_W), F32), pltpu.VMEM((HG_W, HG_W), F32)],
        compiler_params=_cparams(("parallel", "arbitrary")),
        name="hgrn",
    )(r3(hq), r3(lff), r3(hkf), r3(hv), r3(hq), r3(lfb), r3(hkb), r3(hv), *consts)
    return of.reshape(B * S, HG_W), ob.reshape(B * S, HG_W)


LRU_HALF = LRU_W // 2
LRU_HALO = 16


SUBLANES = 8
LRU_SEG = 16
LRU_NSEG = T_LRU // LRU_SEG
LRU_PITCH = LRU_SEG + SUBLANES


def _scan_chunk(a, b, carry, reverse, a_s, b_s, h_s):
    T, C = a.shape
    L, P, G = LRU_SEG, LRU_PITCH, LRU_NSEG // SUBLANES
    outs, carries = [], []
    for slab in range(C // LANES):
        lanes = slice(LANES * slab, LANES * (slab + 1))
        for s in range(LRU_NSEG):
            a_s[slab, P * s:P * s + L, :] = a[L * s:L * (s + 1), lanes]
            b_s[slab, P * s:P * s + L, :] = b[L * s:L * (s + 1), lanes]
        h = [jnp.zeros((SUBLANES, LANES), F32)] * G
        p = [jnp.ones((SUBLANES, LANES), F32)] * G
        h_loc = [[None] * L for _ in range(G)]
        p_loc = [[None] * L for _ in range(G)]
        for v in (range(L - 1, -1, -1) if reverse else range(L)):
            for g in range(G):
                rows = pl.ds(P * SUBLANES * g + v, SUBLANES, stride=P)
                av = a_s[slab, rows, :]
                h[g] = av * h[g] + b_s[slab, rows, :]
                p[g] = av * p[g]
                h_loc[g][v], p_loc[g][v] = h[g], p[g]
        c = carry[:, lanes]
        inflow = [None] * LRU_NSEG
        for s in (range(LRU_NSEG - 1, -1, -1) if reverse else range(LRU_NSEG)):
            g, r = divmod(s, SUBLANES)
            inflow[s] = c
            c = p[g][r:r + 1] * c + h[g][r:r + 1]
        for g in range(G):
            cin = jnp.concatenate(inflow[SUBLANES * g:SUBLANES * (g + 1)], axis=0)
            for v in range(L):
                h_s[slab, pl.ds(P * SUBLANES * g + v, SUBLANES, stride=P), :] = h_loc[g][v] + p_loc[g][v] * cin
        outs.append(jnp.concatenate([h_s[slab, P * s:P * s + L, :] for s in range(LRU_NSEG)], axis=0))
        carries.append(c)
    return jnp.concatenate(outs, axis=1), jnp.concatenate(carries, axis=1)


def _lru_kernel(lx_ref, lg_ref, cw_ref, cb_ref, w_ref, bias_ref, cneg_ref, out_ref, u_ref, ub_ref, hf_ref, hb_ref,
                *stage_refs, S):
    T = T_LRU
    n = S // T
    C = LRU_HALF
    cw = cw_ref[...]
    cb = cb_ref[...]

    def conv_body(i, _):
        t0 = pl.multiple_of(i * T, T)
        cur = lx_ref[pl.ds(t0, T), :].astype(F32)
        ps = pl.multiple_of(jnp.maximum(t0 - LRU_HALO, 0), LRU_HALO)
        prev = jnp.where(i > 0, lx_ref[pl.ds(ps, LRU_HALO), :].astype(F32), 0.0)
        ns = pl.multiple_of(jnp.minimum(t0 + T, S - LRU_HALO), LRU_HALO)
        nxt = jnp.where(i < n - 1, lx_ref[pl.ds(ns, LRU_HALO), :].astype(F32), 0.0)
        ext = jnp.concatenate([prev, cur, nxt], axis=0)
        h = LRU_HALO
        u = (cb + ext[h - 2:h - 2 + T] * cw[0:1] + ext[h - 1:h - 1 + T] * cw[1:2]
             + cur * cw[2:3] + ext[h + 1:h + 1 + T] * cw[3:4])
        u_ref[pl.ds(t0, T), :] = u
        ub_ref[pl.ds(t0, T), :] = u.astype(BF16)
        return 0

    lax.fori_loop(0, n, conv_body, 0)

    def gates(d, i):
        rows = pl.ds(pl.multiple_of(i * T, T), T)
        u = u_ref[rows, :]
        pre = jnp.dot(ub_ref[rows, :], w_ref[:, 2 * C * d:2 * C * (d + 1)], preferred_element_type=F32)
        r = _sigmoid(pre[:, :C] + bias_ref[2 * d:2 * d + 1, :])
        ig = _sigmoid(pre[:, C:] + bias_ref[2 * d + 1:2 * d + 2, :])
        a = jnp.exp2(cneg_ref[d:d + 1, :] * r)
        y = 1.0 - a * a
        b = y * lax.rsqrt(jnp.maximum(y, F32_TINY)) * (ig * u)
        return a, b

    def emit(rows, h):
        g = lg_ref[rows, :].astype(F32)
        out_ref[rows, :] = (h * (g * _sigmoid(g))).astype(BF16)

    def step(i, carry, meet):
        cf, cr = carry
        j = n - 1 - i
        rows_i = pl.ds(pl.multiple_of(i * T, T), T)
        rows_j = pl.ds(pl.multiple_of(j * T, T), T)
        af, bf = gates(0, i)
        ar, br = gates(1, j)
        hf, cf = _scan_chunk(af, bf, cf, False, *stage_refs[0:3])
        hr, cr = _scan_chunk(ar, br, cr, True, *stage_refs[3:6])
        if meet:
            emit(rows_i, hf + hb_ref[rows_i, :])
            emit(rows_j, hf_ref[rows_j, :] + hr)
        else:
            hf_ref[rows_i, :] = hf
            hb_ref[rows_j, :] = hr
        return cf, cr

    zero = jnp.zeros((1, C), F32)
    unroll = 2 if n % 4 == 0 else 1
    carry = lax.fori_loop(0, n // 2, functools.partial(step, meet=False), (zero, zero), unroll=unroll)
    lax.fori_loop(n // 2, n, functools.partial(step, meet=True), carry, unroll=unroll)


def _lru(lx, lg, B, S, cw, cb, w, bias, cneg):
    C = LRU_HALF
    seq = pl.BlockSpec((None, S, C), lambda b, c: (b, 0, c))
    vec = lambda r: pl.BlockSpec((r, C), lambda b, c: (0, c))
    out = pl.pallas_call(
        functools.partial(_lru_kernel, S=S),
        grid=(B, LRU_W // C),
        in_specs=[seq, seq, vec(4), vec(1), pl.BlockSpec((None, C, 4 * C), lambda b, c: (c, 0, 0)), vec(4), vec(2)],
        out_specs=seq,
        out_shape=jax.ShapeDtypeStruct((B, S, LRU_W), BF16),
        scratch_shapes=[pltpu.VMEM((S, C), F32), pltpu.VMEM((S, C), BF16), pltpu.VMEM((S, C), F32), pltpu.VMEM((S, C), F32)]
        + [pltpu.VMEM((C // LANES, LRU_NSEG * LRU_PITCH, LANES), F32)] * 6,
        compiler_params=_cparams(("parallel", "parallel")),
        name="lru",
    )(lx.reshape(B, S, LRU_W), lg.reshape(B, S, LRU_W), cw, cb, w, bias, cneg)
    return out.reshape(B * S, LRU_W)


PV_ROWS = DA_V + 16
M_INIT = -1e30
ATTN_GROUP = 1


def _attn_kernel(lam_ref, qT_ref, qTn_ref, k_ref, vT_ref, dg_ref, ng_ref, out_ref, rhs_ref, m_ref, acc_ref, sa_ref, sb_ref,
                 *, S, tk, lam_init):
    nk = S // tk
    qi = pl.program_id(1)
    cur_q = qi % 2
    nxt_q = 1 - cur_q
    ones = jnp.ones((PV_ROWS - DA_V, tk), BF16)

    def make_rhs(q_ref, slot):
        nslot = LANES // DA_QK
        for hc in range(2 * HEADS):
            qq = q_ref[DA_QK * hc:DA_QK * (hc + 1), :].astype(F32)
            hi = qq.astype(F8)
            lo = (qq - hi.astype(F32)).astype(F8)
            pieces = [hi, hi, lo, lo]
            rhs_ref[slot, hc] = jnp.concatenate([pieces[(s - hc) % nslot] for s in range(nslot)], axis=0)

    def stage(j_next, q_slot, next_ref, j_cur, cur_ref):
        krows = pl.ds(pl.multiple_of(j_next * tk, tk), tk)
        if j_cur is not None:
            vT = vT_ref[:, pl.ds(pl.multiple_of(j_cur * tk, tk), tk)]
        for hc in range(2 * HEADS):
            if hc % ATTN_GROUP == 0:
                for g in range(hc, hc + ATTN_GROUP):
                    next_ref[g] = jnp.dot(k_ref[g, krows, :], rhs_ref[q_slot, g], preferred_element_type=F32)
            if j_cur is not None:
                h = hc // 2
                s = cur_ref[hc]
                m_old = m_ref[hc]
                m_new = jnp.maximum(m_old, jnp.max(s, axis=0, keepdims=True))
                p = jnp.exp2(s - m_new).astype(BF16)
                lhs = jnp.concatenate([vT[DA_V * h:DA_V * (h + 1), :], ones], axis=0)
                acc_ref[hc] = jnp.exp2(m_old - m_new) * acc_ref[hc] + jnp.dot(lhs, p, preferred_element_type=F32)
                m_ref[hc] = m_new

    @pl.when(qi == 0)
    def _():
        make_rhs(qT_ref, 0)
        stage(0, 0, sa_ref, None, None)

    make_rhs(qTn_ref, nxt_q)
    m_ref[...] = jnp.full(m_ref.shape, M_INIT, F32)
    acc_ref[...] = jnp.zeros_like(acc_ref)

    def body(i, _):
        stage(2 * i + 1, cur_q, sb_ref, 2 * i, sa_ref)
        stage(2 * i + 2, cur_q, sa_ref, 2 * i + 1, sb_ref)
        return 0

    lax.fori_loop(0, nk // 2 - 1, body, 0)
    stage(nk - 1, cur_q, sb_ref, nk - 2, sa_ref)
    stage(0, nxt_q, sa_ref, nk - 1, sb_ref)

    lam = lam_ref[0]
    outs = []
    for h in range(HEADS):
        a0 = acc_ref[2 * h]
        a1 = acc_ref[2 * h + 1]
        o = a0[:DA_V] * (1.0 / a0[DA_V:DA_V + 1]) - lam * (a1[:DA_V] * (1.0 / a1[DA_V:DA_V + 1]))
        ms = jnp.mean(o * o, axis=0, keepdims=True)
        outs.append(o * lax.rsqrt(ms + RMS_EPS))
    o = jnp.concatenate(outs, axis=0).T
    g = dg_ref[...].astype(F32)
    out_ref[...] = (o * ng_ref[...] * (1.0 - lam_init) * (g * _sigmoid(g))).astype(BF16)


def _attn(lam, dqT, dk, dvT, dg, ng, B, S, lam_init):
    nq = S // TQ
    tk = min(TK, S // 2)
    out = pl.pallas_call(
        functools.partial(_attn_kernel, S=S, tk=tk, lam_init=lam_init),
        grid=(B, nq),
        in_specs=[
            pl.BlockSpec(memory_space=pltpu.SMEM),
            pl.BlockSpec((None, DA_W, TQ), lambda b, i: (b, 0, i)),
            pl.BlockSpec((None, DA_W, TQ), lambda b, i: (b, 0, jnp.minimum(i + 1, nq - 1))),
            pl.BlockSpec((None, 2 * HEADS, S, LANES), lambda b, i: (b, 0, 0, 0)),
            pl.BlockSpec((None, DA_W, S), lambda b, i: (b, 0, 0)),
            pl.BlockSpec((None, TQ, DA_W), lambda b, i: (b, i, 0)),
            pl.BlockSpec((1, DA_W), lambda b, i: (0, 0)),
        ],
        out_specs=pl.BlockSpec((None, TQ, DA_W), lambda b, i: (b, i, 0)),
        out_shape=jax.ShapeDtypeStruct((B, S, DA_W), BF16),
        scratch_shapes=[
            pltpu.VMEM((2, 2 * HEADS, LANES, TQ), F8),
            pltpu.VMEM((2 * HEADS, 1, TQ), F32),
            pltpu.VMEM((2 * HEADS, PV_ROWS, TQ), F32),
            pltpu.VMEM((2 * HEADS, tk, TQ), F32),
            pltpu.VMEM((2 * HEADS, tk, TQ), F32),
        ],
        compiler_params=_cparams(("parallel", "arbitrary")),
        name="attn",
    )(lam, dqT, dqT, dk, dvT, dg.reshape(B, S, DA_W), ng)
    return out.reshape(B * S, DA_W)


def _outproj_kernel(h_ref, of_ref, ob_ref, hg_ref, olru_ref, oda_ref, p_ref, hgn_ref, ones_ref,
                    wout_ref, gw_ref, pw_ref, fn_ref, out_ref, *, final):
    acc = h_ref[...]
    acc = acc + jnp.dot(olru_ref[...], wout_ref[HG_W:HG_W + LRU_W, :], preferred_element_type=F32)
    acc = acc + jnp.dot(oda_ref[...], wout_ref[HG_W + LRU_W:, :], preferred_element_type=F32)
    ple = jnp.dot(p_ref[...].astype(BF16), pw_ref[...], preferred_element_type=F32)

    o = of_ref[...] + ob_ref[...]
    sq = o * o
    hi = sq.astype(BF16)
    lo = (sq - hi.astype(F32)).astype(BF16)
    ones = ones_ref[...]
    ssum = jnp.dot(hi, ones, preferred_element_type=F32) + jnp.dot(lo, ones, preferred_element_type=F32)
    g = hg_ref[...].astype(F32)
    ohg = (o * lax.rsqrt(ssum * (1.0 / HG_DK) + RMS_EPS) * hgn_ref[...] * (g * _sigmoid(g))).astype(BF16)

    acc = acc + jnp.dot(ohg, wout_ref[0:HG_W, :], preferred_element_type=F32)
    gate = _sigmoid(jnp.dot(acc.astype(BF16), gw_ref[...], preferred_element_type=F32))
    h2 = acc + gate * ple
    if final:
        ms = jnp.mean(h2 * h2, axis=-1, keepdims=True)
        h2 = h2 * lax.rsqrt(ms + RMS_EPS) * fn_ref[...]
    out_ref[...] = h2


def _outproj(h, of, ob, hg, olru, oda, p, l, hgn, ones, wout, gw, pw, fn, final):
    R = h.shape[0]
    tm = min(TM_OUT, R)
    row = lambda w: pl.BlockSpec((tm, w), lambda i: (i, 0))
    full = lambda a: pl.BlockSpec(a.shape, lambda i: (0,) * a.ndim, pipeline_mode=pl.Buffered(1))
    return pl.pallas_call(
        functools.partial(_outproj_kernel, final=final),
        grid=(R // tm,),
        in_specs=[row(D_MODEL), row(HG_W), row(HG_W), row(HG_W), row(LRU_W), row(DA_W),
                  pl.BlockSpec((None, tm, PLE_DIM), lambda i: (l, i, 0)),
                  full(hgn), full(ones), _layer_block(wout, l), _layer_block(gw, l), _layer_block(pw, l), full(fn)],
        out_specs=row(D_MODEL),
        out_shape=jax.ShapeDtypeStruct((R, D_MODEL), F32),
        compiler_params=_cparams(("parallel",)),
        name="outproj",
    )(h, of, ob, hg, olru, oda, p, hgn, ones, wout, gw, pw, fn)


def _rope_tables(S):
    half = ROPE_DIM // 2
    inv = ROPE_THETA ** (-jnp.arange(half, dtype=F32) * 2.0 / ROPE_DIM)
    ang = jnp.arange(S, dtype=F32)[:, None] * inv[None, :]
    cos, sin = jnp.cos(ang), jnp.sin(ang)
    ones = jnp.ones((S, DA_QK - ROPE_DIM), F32)
    zeros = jnp.zeros((S, DA_QK - ROPE_DIM), F32)
    z4 = jnp.zeros((S, half), F32)
    rep = lambda t: jnp.tile(t, (1, LANES // DA_QK))
    return (rep(jnp.concatenate([cos, cos, ones], axis=1)),
            rep(jnp.concatenate([-sin, z4, zeros], axis=1)),
            rep(jnp.concatenate([z4, sin, zeros], axis=1)))


def _lru_gate_weights(wa, wx):
    per_half = LRU_HALF // LRU_BLOCK
    halves = []
    for c in range(LRU_W // LRU_HALF):
        tiles = []
        for d in range(2):
            for w in (wa, wx):
                blocks = [w[d, per_half * c + m] for m in range(per_half)]
                tiles.append(jax.scipy.linalg.block_diag(*blocks))
        halves.append(jnp.concatenate(tiles, axis=1))
    return jnp.stack(halves).astype(BF16)


def _trunk(x, p, B, S, prm):
    R = B * S
    h = x.reshape(R, D_MODEL)
    depth = prm["w_in"].shape[0]
    cos_t, s1_t, s2_t = _rope_tables(S)
    lb_all = jnp.cumsum(jax.nn.softmax(prm["hg_lb"].astype(F32), axis=0), axis=0)
    lb_all = lb_all - lb_all[:1]
    lane = np.arange(HG_W)
    ones_bd = jnp.asarray((lane[:, None] // HG_DK) == (lane[None, :] // HG_DK), BF16)
    p3 = p.reshape(depth, R, PLE_DIM)
    for l in range(depth):
        (hq, hkf, hkb, hv, hg, lff, lfb, lx, lg, dqT, dk, dvT, dg) = _inproj(
            h, B, S, l, prm["norm_g"][l][None, :], prm["w_in_bf"], lb_all[l], cos_t, s1_t, s2_t)

        of, ob = _hgrn(hq, lff, hkf, hv, lfb, hkb, B, S)

        cneg = -LRU_C * LOG2E * jax.nn.softplus(-prm["lru_lam"][l].astype(F32))
        bias = jnp.stack([prm["lru_ba"][l, 0], prm["lru_bx"][l, 0], prm["lru_ba"][l, 1], prm["lru_bx"][l, 1]])
        olru = _lru(lx, lg, B, S, prm["lru_conv_w"][l], prm["lru_conv_b"][l][None, :],
                    _lru_gate_weights(prm["lru_wa"][l], prm["lru_wx"][l]), bias, cneg)

        lam_init = 0.8 - 0.6 * math.exp(-0.3 * l)
        lam = (jnp.exp(jnp.sum(prm["da_lq1"][l].astype(F32) * prm["da_lk1"][l].astype(F32)))
               - jnp.exp(jnp.sum(prm["da_lq2"][l].astype(F32) * prm["da_lk2"][l].astype(F32))) + lam_init)
        oda = _attn(lam.reshape(1), dqT, dk, dvT, dg, jnp.tile(prm["da_norm"][l], HEADS)[None, :], B, S, lam_init)

        h = _outproj(h, of, ob, hg, olru, oda, p3, l, jnp.tile(prm["hg_norm"][l], HEADS)[None, :],
                     ones_bd, prm["w_out_bf"], prm["ple_gate_w_bf"], prm["ple_w_bf"], prm["final_norm"][None, :],
                     final=(l == depth - 1))
    return h.reshape(B, S, D_MODEL)


def kernel(x_prompt, x_sample, p_prompt, p_sample, norm_g, w_in, w_out, hg_lb, hg_norm, lru_conv_w, lru_conv_b,
           lru_wa, lru_ba, lru_wx, lru_bx, lru_lam, da_lq1, da_lk1, da_lq2, da_lk2, da_norm, ple_w, ple_gate_w,
           final_norm):
    prm = dict(norm_g=norm_g, w_in=w_in, w_out=w_out, hg_lb=hg_lb, hg_norm=hg_norm, lru_conv_w=lru_conv_w,
               lru_conv_b=lru_conv_b, lru_wa=lru_wa, lru_ba=lru_ba, lru_wx=lru_wx, lru_bx=lru_bx, lru_lam=lru_lam,
               da_lq1=da_lq1, da_lk1=da_lk1, da_lq2=da_lq2, da_lk2=da_lk2, da_norm=da_norm, ple_w=ple_w,
               ple_gate_w=ple_gate_w, final_norm=final_norm)
    prm.update(w_in_bf=w_in.astype(BF16), w_out_bf=w_out.astype(BF16), ple_gate_w_bf=ple_gate_w.astype(BF16),
               ple_w_bf=ple_w.astype(BF16))
    outs = []
    for x, p in ((x_prompt, p_prompt), (x_sample, p_sample)):
        B, S, _ = x.shape
        outs.append(_trunk(x, p, B, S, prm))
    return tuple(outs)
```
